```python
import math, functools
import jax, jax.numpy as jnp
from jax import lax
import numpy as np

D_MODEL = 2048
BATCH = 1
SEQ = 8192
DEPTH = 1
DEC_BATCH = 128
DEC_SEQ = 8
PAST_LEN = 16384
PAGE_SIZE = 128

N_META = 16
M_HEADS = 8
M_DK = D_MODEL // 16
M_DV = D_MODEL // 8
M_CHUNK = 64
FORGET_BIAS = 3.0
A_HEADS = D_MODEL // 128
A_NOPE = 128
A_ROPE = 64
A_VD = 128
Q_LORA = D_MODEL // 4
KV_LORA = D_MODEL // 8
ROPE_BASE = 10000.0
Q_BLOCK = 128
ATTN_SCALE = (A_NOPE + A_ROPE) ** -0.5
D_FF = 4 * D_MODEL
ALPHA = (2.0 * DEPTH) ** 0.25
BETA = (8.0 * DEPTH) ** -0.25
LN_EPS = 1e-5
RMS_EPS = 1e-6
SECTION_WIDTHS = (M_HEADS * M_DK, M_HEADS * M_DK, M_HEADS * M_DV, M_HEADS * M_DV, M_HEADS, M_HEADS,
                  Q_LORA, KV_LORA + A_ROPE, 2 * D_MODEL)
SPLIT_POINTS = tuple(int(s) for s in np.cumsum(SECTION_WIDTHS)[:-1])
D_IN = int(sum(SECTION_WIDTHS))
F_GATE_OFFSET = SPLIT_POINTS[4]

kernel_name = "hybrid_mlstm_mla_gated_decoder_step"


def layer_norm(x, g, b):
    xf = x.astype(jnp.float32)
    mu = jnp.mean(xf, axis=-1, keepdims=True)
    var = jnp.mean(jnp.square(xf - mu), axis=-1, keepdims=True)
    return ((xf - mu) * lax.rsqrt(var + LN_EPS) * g.astype(jnp.float32) + b.astype(jnp.float32)).astype(x.dtype)


def rms_norm(x, g):
    xf = x.astype(jnp.float32)
    return (xf * lax.rsqrt(jnp.mean(jnp.square(xf), axis=-1, keepdims=True) + RMS_EPS) * g.astype(jnp.float32)).astype(x.dtype)


def rope(x, pos):
    half = A_ROPE // 2
    inv = ROPE_BASE ** (-jnp.arange(half, dtype=jnp.float32) / half)
    ang = pos.astype(jnp.float32)[:, None] * inv
    ang = ang.reshape((ang.shape[0],) + (1,) * (x.ndim - 3) + (half,))
    cos, sin = jnp.cos(ang), jnp.sin(ang)
    xf = x.astype(jnp.float32)
    x1, x2 = xf[..., :half], xf[..., half:]
    return jnp.concatenate([x1 * cos - x2 * sin, x2 * cos + x1 * sin], axis=-1).astype(x.dtype)


def mlstm_chunk(q, k, v, ig, lf, C0, n0, m0):
    L = q.shape[2]
    b = jnp.cumsum(lf, axis=-1)
    causal = jnp.tril(jnp.ones((L, L), dtype=bool))
    dlog = jnp.where(causal, b[..., :, None] - b[..., None, :] + ig[..., None, :], -jnp.inf)
    m = jnp.maximum(b + m0[..., None], jnp.max(dlog, axis=-1))
    w = jnp.exp(dlog - m[..., None])
    inter = jnp.exp(b + m0[..., None] - m)
    s = jnp.einsum('bhtd,bhsd->bhts', q, k) * w
    num = jnp.einsum('bhts,bhsv->bhtv', s, v) + inter[..., None] * jnp.einsum('bhtd,bhdv->bhtv', q, C0)
    nq = jnp.sum(s, axis=-1) + inter * jnp.einsum('bhtd,bhd->bht', q, n0)
    h = num / jnp.maximum(jnp.abs(nq), jnp.exp(-m))[..., None]
    m_end = m[..., -1]
    w_end = jnp.exp(b[..., -1:] - b + ig - m_end[..., None])
    dec = jnp.exp(b[..., -1] + m0 - m_end)
    C = dec[..., None, None] * C0 + jnp.einsum('bhs,bhsd,bhsv->bhdv', w_end, k, v)
    n = dec[..., None] * n0 + jnp.einsum('bhs,bhsd->bhd', w_end, k)
    return h, C, n, m_end


def mlstm_prompt(q, k, v, ig, lf):
    B, H, L, _ = q.shape
    C0 = jnp.zeros((B, H, M_DK, M_DV), jnp.float32)
    n0 = jnp.zeros((B, H, M_DK), jnp.float32)
    m0 = jnp.zeros((B, H), jnp.float32)
    h_meta, C, n, m = mlstm_chunk(q[:, :, :N_META], k[:, :, :N_META], v[:, :, :N_META],
                                  ig[:, :, :N_META], lf[:, :, :N_META], C0, n0, m0)
    nc = (L - N_META) // M_CHUNK

    def to_chunks(t):
        r = t[:, :, N_META:]
        return jnp.moveaxis(r.reshape((B, H, nc, M_CHUNK) + r.shape[3:]), 2, 0)

    def step(carry, xs):
        h, Cn, nn, mn = mlstm_chunk(*xs, *carry)
        return (Cn, nn, mn), h

    (C, n, m), hs = lax.scan(step, (C, n, m), tuple(to_chunks(t) for t in (q, k, v, ig, lf)))
    hs = jnp.moveaxis(hs, 0, 2).reshape(B, H, L - N_META, M_DV)
    return jnp.concatenate([h_meta, hs], axis=2), C, n, m


def mlstm_readout(h, uo, mh_g):
    B, H, L, DV = h.shape
    mu = jnp.mean(h, axis=-1, keepdims=True)
    var = jnp.mean(jnp.square(h - mu), axis=-1, keepdims=True)
    hn = ((h - mu) * lax.rsqrt(var + LN_EPS)).transpose(0, 2, 1, 3).reshape(B, L, H * DV)
    return (hn * mh_g.astype(jnp.float32) * jax.nn.sigmoid(uo.astype(jnp.float32))).astype(uo.dtype)


def mla_project(uaq, uakv, pos, q_norm_g, kv_norm_g, w_uq, w_uk):
    B, L, _ = uaq.shape
    q = (rms_norm(uaq, q_norm_g) @ w_uq).reshape(B, L, A_HEADS, A_NOPE + A_ROPE)
    q_nope, q_rope = q[..., :A_NOPE], rope(q[..., A_NOPE:], pos)
    c_kv = rms_norm(uakv[..., :KV_LORA], kv_norm_g)
    k_rope = rope(uakv[..., KV_LORA:], pos)
    q_lat = jnp.einsum('blhd,rhd->blhr', q_nope, w_uk)
    return q_lat, q_rope, c_kv, k_rope


def attend(q_lat, q_rope, c, kr, q_pos, k_pos):
    s = jnp.einsum('thr,kr->htk', q_lat, c) + jnp.einsum('thp,kp->htk', q_rope, kr)
    s = s.astype(jnp.float32) * ATTN_SCALE
    s = jnp.where(k_pos[None, None, :] <= q_pos[None, :, None], s, -jnp.inf)
    p = jax.nn.softmax(s, axis=-1)
    return jnp.einsum('htk,kr->thr', p.astype(c.dtype), c)


def prompt_attention(q_lat, q_rope, c_kv, k_rope):
    B, L = q_lat.shape[:2]
    nb = -(-L // Q_BLOCK)
    pad = ((0, 0), (0, nb * Q_BLOCK - L), (0, 0), (0, 0))
    ql_p, qr_p = jnp.pad(q_lat, pad), jnp.pad(q_rope, pad)
    k_pos = jnp.arange(L, dtype=jnp.int32)
    batched = jax.vmap(attend, in_axes=(0, 0, 0, 0, None, None))

    def block(i):
        s0 = i * Q_BLOCK
        ql = lax.dynamic_slice_in_dim(ql_p, s0, Q_BLOCK, axis=1)
        qr = lax.dynamic_slice_in_dim(qr_p, s0, Q_BLOCK, axis=1)
        return batched(ql, qr, c_kv, k_rope, s0 + jnp.arange(Q_BLOCK, dtype=jnp.int32), k_pos)

    o = lax.map(block, jnp.arange(nb, dtype=jnp.int32))
    return jnp.moveaxis(o, 0, 1).reshape(B, nb * Q_BLOCK, A_HEADS, KV_LORA)[:, :L]


def paged_attention(q_lat, q_rope, c_new, kr_new, cache_latent, cache_krope, page_table):
    T = q_lat.shape[1]
    k_pos = jnp.arange(PAST_LEN + T, dtype=jnp.int32)
    q_pos = PAST_LEN + jnp.arange(T, dtype=jnp.int32)

    def one(args):
        ql, qr, cn, kn, pages = args
        c = jnp.concatenate([cache_latent[pages].reshape(-1, KV_LORA).astype(cn.dtype), cn], axis=0)
        kr = jnp.concatenate([cache_krope[pages].reshape(-1, A_ROPE).astype(kn.dtype), kn], axis=0)
        return attend(ql, qr, c, kr, q_pos, k_pos)

    return lax.map(one, (q_lat, q_rope, c_new, kr_new, page_table))


def heads(t, d):
    B, L, _ = t.shape
    return t.reshape(B, L, M_HEADS, d).transpose(0, 2, 1, 3).astype(jnp.float32)


def trunk_layer(x, pos, run_mlstm, run_attn, w_in, b_in, mh_g, q_norm_g, kv_norm_g, w_uq, w_uk, w_uv,
                w_br_m, w_br_a, w_out, ln1_g, ln1_b, w_up, w_down, ln2_g, ln2_b):
    B, L, _ = x.shape
    uq, uk, uv, uo, ui, uf, uaq, uakv, ug = jnp.split(x @ w_in + b_in, SPLIT_POINTS, axis=-1)
    q, k, v = heads(uq, M_DK), heads(uk, M_DK) * (M_DK ** -0.5), heads(uv, M_DV)
    ig = jnp.swapaxes(ui.astype(jnp.float32), 1, 2)
    lf = jax.nn.log_sigmoid(jnp.swapaxes(uf.astype(jnp.float32), 1, 2))
    h, C, n, m = run_mlstm(q, k, v, ig, lf)
    y_m = mlstm_readout(h, uo, mh_g) @ w_br_m
    q_lat, q_rope, c_kv, k_rope = mla_project(uaq, uakv, pos, q_norm_g, kv_norm_g, w_uq, w_uk)
    o_lat = run_attn(q_lat, q_rope, c_kv, k_rope)
    y_a = jnp.einsum('blhr,rhe->blhe', o_lat, w_uv).reshape(B, L, A_HEADS * A_VD) @ w_br_a
    g_m, g_a = jnp.split(jax.nn.sigmoid(ug), 2, axis=-1)
    mix = (g_m * y_m + g_a * y_a) @ w_out
    x = layer_norm(ALPHA * x + mix, ln1_g, ln1_b)
    x = layer_norm(ALPHA * x + jnp.square(jax.nn.relu(x @ w_up)) @ w_down, ln2_g, ln2_b)
    return x, (c_kv, k_rope, C, n, m)


def setup_inputs(seed: int = 0) -> dict:
    key = jax.random.key(seed)
    ks = jax.random.split(key, 32)
    f32 = jnp.float32
    n_pages = PAST_LEN // PAGE_SIZE
    n_pool = (DEC_BATCH * n_pages * 5) // 4

    def w(k, shape, fan_in, scale=1.0):
        return jax.random.normal(k, shape, f32) * (fan_in ** -0.5) * scale

    def gain(k, shape):
        return 1.0 + 0.02 * jax.random.normal(k, shape, f32)

    b_in = 0.02 * jax.random.normal(ks[10], (DEPTH, D_IN), f32)
    b_in = b_in.at[:, F_GATE_OFFSET:F_GATE_OFFSET + M_HEADS].add(FORGET_BIAS)
    perm = jax.random.permutation(ks[7], n_pool)[:DEC_BATCH * n_pages]
    return {
        "x_prompt": jax.random.normal(ks[0], (BATCH, SEQ, D_MODEL), f32),
        "x_sample": jax.random.normal(ks[1], (DEC_BATCH, DEC_SEQ, D_MODEL), f32),
        "cache_latent": jax.random.normal(ks[2], (DEPTH, n_pool, PAGE_SIZE, KV_LORA), f32),
        "cache_krope": jax.random.normal(ks[3], (DEPTH, n_pool, PAGE_SIZE, A_ROPE), f32),
        "state_C": 0.5 * jax.random.normal(ks[4], (DEPTH, DEC_BATCH, M_HEADS, M_DK, M_DV), f32),
        "state_n": 0.5 * jax.random.normal(ks[5], (DEPTH, DEC_BATCH, M_HEADS, M_DK), f32),
        "state_m": jax.random.normal(ks[6], (DEPTH, DEC_BATCH, M_HEADS), f32),
        "page_table": perm.reshape(DEC_BATCH, n_pages).astype(jnp.int32),
        "meta": jax.random.normal(ks[8], (N_META, D_MODEL), f32),
        "w_in": w(ks[9], (DEPTH, D_MODEL, D_IN), D_MODEL),
        "b_in": b_in,
        "mh_g": gain(ks[11], (DEPTH, M_HEADS * M_DV)),
        "q_norm_g": gain(ks[12], (DEPTH, Q_LORA)),
        "kv_norm_g": gain(ks[13], (DEPTH, KV_LORA)),
        "w_uq": w(ks[14], (DEPTH, Q_LORA, A_HEADS * (A_NOPE + A_ROPE)), Q_LORA),
        "w_uk": w(ks[15], (DEPTH, KV_LORA, A_HEADS, A_NOPE), KV_LORA),
        "w_uv": w(ks[16], (DEPTH, KV_LORA, A_HEADS, A_VD), KV_LORA),
        "w_br_m": w(ks[17], (DEPTH, M_HEADS * M_DV, D_MODEL), M_HEADS * M_DV),
        "w_br_a": w(ks[18], (DEPTH, A_HEADS * A_VD, D_MODEL), A_HEADS * A_VD),
        "w_out": w(ks[19], (DEPTH, D_MODEL, D_MODEL), D_MODEL, BETA),
        "ln1_g": gain(ks[20], (DEPTH, D_MODEL)),
        "ln1_b": 0.02 * jax.random.normal(ks[21], (DEPTH, D_MODEL), f32),
        "w_up": w(ks[22], (DEPTH, D_MODEL, D_FF), D_MODEL),
        "w_down": w(ks[23], (DEPTH, D_FF, D_MODEL), D_FF, BETA),
        "ln2_g": gain(ks[24], (DEPTH, D_MODEL)),
        "ln2_b": 0.02 * jax.random.normal(ks[25], (DEPTH, D_MODEL), f32),
    }


def reference(x_prompt, x_sample, cache_latent, cache_krope, state_C, state_n, state_m, page_table, meta,
              w_in, b_in, mh_g, q_norm_g, kv_norm_g, w_uq, w_uk, w_uv, w_br_m, w_br_a, w_out,
              ln1_g, ln1_b, w_up, w_down, ln2_g, ln2_b):
    B = x_prompt.shape[0]
    T = x_sample.shape[1]
    xp = jnp.concatenate([jnp.broadcast_to(meta[None], (B, N_META, D_MODEL)).astype(x_prompt.dtype), x_prompt], axis=1)
    pos_p = jnp.arange(xp.shape[1], dtype=jnp.int32)
    pos_s = PAST_LEN + jnp.arange(T, dtype=jnp.int32)
    xs = x_sample
    layer_w = (w_in, b_in, mh_g, q_norm_g, kv_norm_g, w_uq, w_uk, w_uv, w_br_m, w_br_a, w_out,
               ln1_g, ln1_b, w_up, w_down, ln2_g, ln2_b)
    p_states, s_states = [], []
    for l in range(DEPTH):
        lw = [wt[l] for wt in layer_w]
        xp, st_p = trunk_layer(xp, pos_p, mlstm_prompt, prompt_attention, *lw)
        run_mlstm_s = functools.partial(mlstm_chunk, C0=state_C[l].astype(jnp.float32),
                                        n0=state_n[l].astype(jnp.float32), m0=state_m[l].astype(jnp.float32))
        run_attn_s = functools.partial(paged_attention, cache_latent=cache_latent[l],
                                       cache_krope=cache_krope[l], page_table=page_table)
        xs, st_s = trunk_layer(xs, pos_s, run_mlstm_s, run_attn_s, *lw)
        p_states.append(st_p)
        s_states.append(st_s)
    p_latent = jnp.stack([s[0] for s in p_states])
    p_krope = jnp.stack([s[1] for s in p_states])
    p_C = jnp.stack([s[2] for s in p_states])
    p_n = jnp.stack([s[3] for s in p_states])
    p_m = jnp.stack([s[4] for s in p_states])
    s_latent = jnp.stack([s[0] for s in s_states])
    s_krope = jnp.stack([s[1] for s in s_states])
    s_C = jnp.stack([s[2] for s in s_states])
    s_n = jnp.stack([s[3] for s in s_states])
    s_m = jnp.stack([s[4] for s in s_states])
    y_prompt = xp[:, N_META:]
    return (y_prompt, xs, p_latent, p_krope, p_C, p_n, p_m, s_latent, s_krope, s_C, s_n, s_m)
```

```python
import functools

import numpy as np
import jax
import jax.numpy as jnp
from jax import lax
from jax.experimental import pallas as pl
from jax.experimental.pallas import tpu as pltpu

F32 = jnp.float32
BF16 = jnp.bfloat16

D_MODEL = 2048
SEQ = 8192
DEC_BATCH = 128
DEC_SEQ = 8
PAST_LEN = 16384
PAGE_SIZE = 128
N_META = 16
M_HEADS = 8
M_DK = 128
M_DV = 256
A_HEADS = 16
A_NOPE = 128
A_ROPE = 64
A_VD = 128
Q_LORA = 512
KV_LORA = 256
ROPE_BASE = 10000.0
ATTN_SCALE = (A_NOPE + A_ROPE) ** -0.5
D_FF = 4 * D_MODEL
ALPHA = 2.0 ** 0.25
LN_EPS = 1e-5
RMS_EPS = 1e-6

OFF_Q, OFF_K, OFF_V, OFF_O, OFF_I, OFF_F, OFF_AQ, OFF_CKV, OFF_KR, OFF_G = (
    0, 1024, 2048, 4096, 6144, 6152, 6160, 6672, 6928, 6992)
D_IN = 11088

ROW_S0 = SEQ
ROW_M0 = SEQ + DEC_BATCH * DEC_SEQ
ROWS = 9728
TM = 512
LANE = 128
ROPE_PAD = LANE

M_CHUNK_P = 256
SEQ_PER_STEP = 2
TQ = 128
TK = 512
PAGES_PER_STEP = 16
VMEM_LIMIT = 56 * 1024 * 1024


def _cparams(sem):
    return pltpu.CompilerParams(dimension_semantics=sem, vmem_limit_bytes=VMEM_LIMIT)


def _mm_kernel(x_ref, w_ref, b_ref, s_ref, o_ref, xb_ref, *, act):
    @pl.when(pl.program_id(1) == 0)
    def _():
        xb_ref[...] = x_ref[...].astype(BF16)

    y = jnp.dot(xb_ref[...], w_ref[...], preferred_element_type=F32)
    y = (y + b_ref[...]) * s_ref[...]
    if act == "sigmoid":
        y = jax.nn.sigmoid(y)
    o_ref[...] = y.astype(o_ref.dtype)


def _matmul(x, w, b, s, *, act, out_dtype, tn, name):
    m, k = x.shape
    n = w.shape[1]
    return pl.pallas_call(
        functools.partial(_mm_kernel, act=act),
        grid=(m // TM, n // tn),
        in_specs=[
            pl.BlockSpec((TM, k), lambda i, j: (i, 0)),
            pl.BlockSpec((k, tn), lambda i, j: (0, j)),
            pl.BlockSpec((1, tn), lambda i, j: (0, j)),
            pl.BlockSpec((1, tn), lambda i, j: (0, j)),
        ],
        out_specs=pl.BlockSpec((TM, tn), lambda i, j: (i, j)),
        out_shape=jax.ShapeDtypeStruct((m, n), out_dtype),
        scratch_shapes=[pltpu.VMEM((TM, k), BF16)],
        compiler_params=_cparams(("parallel", "arbitrary")),
        name=name,
    )(x, w, b, s)


T_AQ, T_CKV, T_KR, T_KROT, T_G, T_END = 0, 512, 768, 896, 1024, 1152


def _tail_kernel(x_ref, w_ref, b_ref, qg_ref, kg_ref, cos_ref, sin_ref,
                 qn_ref, cf_ref, cb_ref, rf_ref, rb_ref, g_ref):
    xb = x_ref[...].astype(BF16)
    acc = jnp.dot(xb, w_ref[...], preferred_element_type=F32) + b_ref[...]
    aq = acc[:, T_AQ:T_CKV]
    qn = aq * lax.rsqrt(jnp.mean(aq * aq, axis=-1, keepdims=True) + RMS_EPS) * qg_ref[...]
    qn_ref[...] = qn.astype(BF16)
    ckv = acc[:, T_CKV:T_KR]
    c = ckv * lax.rsqrt(jnp.mean(ckv * ckv, axis=-1, keepdims=True) + RMS_EPS) * kg_ref[...]
    cf_ref[...] = c
    cb_ref[...] = c.astype(BF16)
    kr = acc[:, T_KR:T_KROT] * cos_ref[...] + acc[:, T_KROT:T_G] * sin_ref[...]
    rf_ref[...] = kr
    rb_ref[...] = kr.astype(BF16)
    g_ref[...] = acc[:, T_G:T_END]


def _tail(x, w, b, qg, kg, cos, sin):
    row = lambda n: pl.BlockSpec((TM, n), lambda i: (i, 0))
    full = lambda a: pl.BlockSpec(a.shape, lambda i: (0,) * a.ndim)
    sd = jax.ShapeDtypeStruct
    return pl.pallas_call(
        _tail_kernel,
        grid=(ROWS // TM,),
        in_specs=[row(D_MODEL), full(w), full(b), full(qg), full(kg), row(ROPE_PAD), row(ROPE_PAD)],
        out_specs=[row(Q_LORA), row(KV_LORA), row(KV_LORA), row(ROPE_PAD), row(ROPE_PAD), row(LANE)],
        out_shape=[sd((ROWS, Q_LORA), BF16), sd((ROWS, KV_LORA), F32), sd((ROWS, KV_LORA), BF16),
                   sd((ROWS, ROPE_PAD), F32), sd((ROWS, ROPE_PAD), BF16), sd((ROWS, LANE), F32)],
        compiler_params=_cparams(("parallel",)),
        name="inproj_tail",
    )(x, w, b, qg, kg, cos, sin)


def _log_sigmoid(x):
    return jnp.minimum(x, 0.0) - jnp.log1p(jnp.exp(-jnp.abs(x)))


def _mlstm_head(q, k, v, ig_c, ig_r, lf_c, lf_r, c0, n0, m0, length):
    ti = lax.broadcasted_iota(jnp.int32, (length, length), 0)
    si = lax.broadcasted_iota(jnp.int32, (length, length), 1)
    causal = si <= ti
    b_c = jnp.sum(jnp.where(causal, lf_r, 0.0), axis=1, keepdims=True)
    b_r = jnp.sum(jnp.where(ti <= si, lf_c, 0.0), axis=0, keepdims=True)
    dlog = jnp.where(causal, b_c - b_r + ig_r, -jnp.inf)
    m = jnp.maximum(b_c + m0, jnp.max(dlog, axis=1, keepdims=True))
    w = jnp.exp(dlog - m)
    inter = jnp.exp(b_c + m0 - m)
    s = lax.dot_general(q, k, (((1,), (1,)), ((), ())), preferred_element_type=F32) * w
    qc = jnp.dot(q, c0.astype(BF16), preferred_element_type=F32)
    num = jnp.dot(s.astype(BF16), v, preferred_element_type=F32) + inter * qc
    qn = jnp.sum(q.astype(F32) * n0, axis=1, keepdims=True)
    nq = jnp.sum(s, axis=1, keepdims=True) + inter * qn
    h = num / jnp.maximum(jnp.abs(nq), jnp.exp(-m))
    m_end = m[length - 1:length, :]
    b_end = b_c[length - 1:length, :]
    w_end = jnp.exp(b_end - b_c + ig_c - m_end)
    dec = jnp.exp(b_end + m0 - m_end)
    kw = k.astype(F32) * w_end
    c = dec * c0 + lax.dot_general(kw.astype(BF16), v, (((0,), (0,)), ((), ())),
                                   preferred_element_type=F32)
    n = dec * n0 + jnp.sum(kw, axis=0, keepdims=True)
    return h, c, n, m_end


def _mlstm_kernel(q_ref, k_ref, v_ref, g_ref, og_ref, mhg_ref, c0_ref, n0_ref, m0_ref, yin_ref,
                  y_ref, c_ref, n_ref, m_ref, cs_ref, ns_ref, ms_ref, *, length, nseq, carry):
    del yin_ref
    step = pl.program_id(0)
    if carry:
        @pl.when(step == 0)
        def _():
            cs_ref[...] = c0_ref[...]
            ns_ref[...] = n0_ref[...]
            ms_ref[...] = m0_ref[...]
        c_src, n_src, m_src = cs_ref, ns_ref, ms_ref
    else:
        c_src, n_src, m_src = c0_ref, n0_ref, m0_ref

    for b in range(nseq):
        r0 = b * length
        g = g_ref[r0:r0 + length, :]
        gt = g.T
        lfc_all = _log_sigmoid(g)
        lfr_all = _log_sigmoid(gt)
        for h in range(M_HEADS):
            q = q_ref[r0:r0 + length, h * M_DK:(h + 1) * M_DK]
            k = k_ref[r0:r0 + length, h * M_DK:(h + 1) * M_DK]
            v = v_ref[r0:r0 + length, h * M_DV:(h + 1) * M_DV]
            hh, c, n, m_end = _mlstm_head(
                q, k, v,
                g[:, h:h + 1], gt[h:h + 1, :],
                lfc_all[:, M_HEADS + h:M_HEADS + h + 1], lfr_all[M_HEADS + h:M_HEADS + h + 1, :],
                c_src[b, h], n_src[b, h:h + 1, :], m_src[b, h:h + 1, :], length)
            mu = jnp.mean(hh, axis=1, keepdims=True)
            d = hh - mu
            var = jnp.mean(d * d, axis=1, keepdims=True)
            hn = d * lax.rsqrt(var + LN_EPS)
            cols = slice(h * M_DV, (h + 1) * M_DV)
            y_ref[r0:r0 + length, cols] = (hn * mhg_ref[:, cols] * og_ref[r0:r0 + length, cols]).astype(BF16)
            if carry:
                cs_ref[b, h] = c
                ns_ref[b, h:h + 1, :] = n
                ms_ref[b, h:h + 1, :] = m_end
            else:
                c_ref[b, h] = c
                n_ref[b, h:h + 1, :] = n
                m_ref[b, h:h + 1, :] = m_end

    if carry:
        @pl.when(step == pl.num_programs(0) - 1)
        def _():
            c_ref[...] = cs_ref[...]
            n_ref[...] = ns_ref[...]
            m_ref[...] = ms_ref[...]


def _mlstm(qkv, gates, og, mhg, c0, n0, m0, y_in, *, length, nseq, nsteps, row0, carry, name):
    rows = nseq * length
    rb0 = row0 // rows
    nstate = c0.shape[0]
    if carry:
        st = lambda *tail: pl.BlockSpec((nstate,) + tail, lambda s: (0,) * (1 + len(tail)))
    else:
        st = lambda *tail: pl.BlockSpec((nseq,) + tail, lambda s: (s,) + (0,) * len(tail))
    sd = jax.ShapeDtypeStruct
    return pl.pallas_call(
        functools.partial(_mlstm_kernel, length=length, nseq=nseq, carry=carry),
        grid=(nsteps,),
        in_specs=[
            pl.BlockSpec((rows, M_HEADS * M_DK), lambda s: (rb0 + s, 0)),
            pl.BlockSpec((rows, M_HEADS * M_DK), lambda s: (rb0 + s, 1)),
            pl.BlockSpec((rows, M_HEADS * M_DV), lambda s: (rb0 + s, 1)),
            pl.BlockSpec((rows, LANE), lambda s: (rb0 + s, 0)),
            pl.BlockSpec((rows, M_HEADS * M_DV), lambda s: (rb0 + s, 0)),
            pl.BlockSpec((1, M_HEADS * M_DV), lambda s: (0, 0)),
            st(M_HEADS, M_DK, M_DV), st(M_HEADS, M_DK), st(M_HEADS, 1),
            pl.BlockSpec(memory_space=pl.ANY),
        ],
        out_specs=[
            pl.BlockSpec((rows, M_HEADS * M_DV), lambda s: (rb0 + s, 0)),
            st(M_HEADS, M_DK, M_DV), st(M_HEADS, M_DK), st(M_HEADS, 1),
        ],
        out_shape=[sd(y_in.shape, BF16), sd(c0.shape, F32), sd(n0.shape, F32), sd(m0.shape, F32)],
        scratch_shapes=[pltpu.VMEM((nstate, M_HEADS, M_DK, M_DV), F32),
                        pltpu.VMEM((nstate, M_HEADS, M_DK), F32),
                        pltpu.VMEM((nstate, M_HEADS, 1), F32)] if carry else
                       [pltpu.VMEM((8, LANE), F32)] * 3,
        input_output_aliases={9: 0},
        compiler_params=_cparams(("arbitrary",)),
        name=name,
    )(qkv, qkv, qkv, gates, og, mhg, c0, n0, m0, y_in)


QW = A_NOPE + 2 * ROPE_PAD


def _qproj_kernel(qn_ref, wq_ref, wuk_ref, cos_ref, sin_ref, ql_ref, qr_ref):
    q = jnp.dot(qn_ref[...], wq_ref[...], preferred_element_type=F32)
    ql = jnp.dot(q[:, :A_NOPE].astype(BF16), wuk_ref[...], preferred_element_type=F32)
    ql_ref[...] = ql.astype(BF16)
    qr = q[:, A_NOPE:A_NOPE + ROPE_PAD] * cos_ref[...] + q[:, A_NOPE + ROPE_PAD:] * sin_ref[...]
    qr_ref[...] = qr.astype(BF16)


def _qproj(qn, wq, wuk, cos, sin):
    sd = jax.ShapeDtypeStruct
    return pl.pallas_call(
        _qproj_kernel,
        grid=(ROWS // TM, A_HEADS),
        in_specs=[
            pl.BlockSpec((TM, Q_LORA), lambda i, h: (i, 0)),
            pl.BlockSpec((None, Q_LORA, QW), lambda i, h: (h, 0, 0)),
            pl.BlockSpec((None, A_NOPE, KV_LORA), lambda i, h: (h, 0, 0)),
            pl.BlockSpec((TM, ROPE_PAD), lambda i, h: (i, 0)),
            pl.BlockSpec((TM, ROPE_PAD), lambda i, h: (i, 0)),
        ],
        out_specs=[pl.BlockSpec((TM, KV_LORA), lambda i, h: (i, h)),
                   pl.BlockSpec((TM, ROPE_PAD), lambda i, h: (i, h))],
        out_shape=[sd((ROWS, A_HEADS * KV_LORA), BF16), sd((ROWS, A_HEADS * ROPE_PAD), BF16)],
        compiler_params=_cparams(("parallel", "arbitrary")),
        name="mla_qproj",
    )(qn, wq, wuk, cos, sin)


def _stack_heads(ref, width):
    return jnp.concatenate([ref[:, h * width:(h + 1) * width] for h in range(A_HEADS)], axis=0)


def _prompt_attn_kernel(qi_ref, kj_ref, first_ref, last_ref, qpos_ref,
                        ql_ref, qr_ref, kl_ref, kr_ref, ml_ref, mr_ref, oin_ref,
                        o_ref, qls_ref, qrs_ref, m_ref, l_ref, acc_ref):
    del oin_ref
    s_idx = pl.program_id(0)
    rows = A_HEADS * TQ
    qpos = qpos_ref[s_idx] + (lax.broadcasted_iota(jnp.int32, (rows, 1), 0) & (TQ - 1))

    @pl.when(first_ref[s_idx] == 1)
    def _():
        qls_ref[...] = _stack_heads(ql_ref, KV_LORA)
        qrs_ref[...] = _stack_heads(qr_ref, ROPE_PAD)
        kl = ml_ref[...]
        s = (lax.dot_general(qls_ref[...], kl, (((1,), (1,)), ((), ())), preferred_element_type=F32)
             + lax.dot_general(qrs_ref[...], mr_ref[...], (((1,), (1,)), ((), ())),
                               preferred_element_type=F32)) * ATTN_SCALE
        kpos = lax.broadcasted_iota(jnp.int32, (1, N_META), 1)
        s = jnp.where(kpos <= qpos, s, -jnp.inf)
        m = jnp.max(s, axis=1, keepdims=True)
        p = jnp.exp(s - m)
        m_ref[...] = m
        l_ref[...] = jnp.sum(p, axis=1, keepdims=True)
        acc_ref[...] = jnp.dot(p.astype(BF16), kl, preferred_element_type=F32)

    kl = kl_ref[...]
    s = (lax.dot_general(qls_ref[...], kl, (((1,), (1,)), ((), ())), preferred_element_type=F32)
         + lax.dot_general(qrs_ref[...], kr_ref[...], (((1,), (1,)), ((), ())),
                           preferred_element_type=F32)) * ATTN_SCALE
    kpos = N_META + kj_ref[s_idx] * TK + lax.broadcasted_iota(jnp.int32, (1, TK), 1)
    s = jnp.where(kpos <= qpos, s, -jnp.inf)
    m_old = m_ref[...]
    m_new = jnp.maximum(m_old, jnp.max(s, axis=1, keepdims=True))
    alpha = jnp.exp(m_old - m_new)
    p = jnp.exp(s - m_new)
    m_ref[...] = m_new
    l_ref[...] = alpha * l_ref[...] + jnp.sum(p, axis=1, keepdims=True)
    acc_ref[...] = alpha * acc_ref[...] + jnp.dot(p.astype(BF16), kl, preferred_element_type=F32)

    @pl.when(last_ref[s_idx] == 1)
    def _():
        o = acc_ref[...] / l_ref[...]
        for h in range(A_HEADS):
            o_ref[:, h * KV_LORA:(h + 1) * KV_LORA] = o[h * TQ:(h + 1) * TQ, :].astype(BF16)


def _prompt_schedule():
    qi, kj, first, last, qpos = [], [], [], [], []
    for i in range(SEQ // TQ):
        nk = (i * TQ + TQ - 1) // TK + 1
        for j in range(nk):
            qi.append(i); kj.append(j); first.append(int(j == 0)); last.append(int(j == nk - 1))
            qpos.append(N_META + i * TQ)
    qi.append(ROW_M0 // TQ); kj.append(0); first.append(1); last.append(1); qpos.append(0)
    return [jnp.asarray(np.asarray(a, np.int32)) for a in (qi, kj, first, last, qpos)]


def _prompt_attn(ql, qr, kl, kr, o_in):
    sched = _prompt_schedule()
    nsteps = int(sched[0].shape[0])
    rows = A_HEADS * TQ
    mblk = ROW_M0 // N_META
    grid_spec = pltpu.PrefetchScalarGridSpec(
        num_scalar_prefetch=5,
        grid=(nsteps,),
        in_specs=[
            pl.BlockSpec((TQ, A_HEADS * KV_LORA), lambda s, qi, kj, f, l, p: (qi[s], 0)),
            pl.BlockSpec((TQ, A_HEADS * ROPE_PAD), lambda s, qi, kj, f, l, p: (qi[s], 0)),
            pl.BlockSpec((TK, KV_LORA), lambda s, qi, kj, f, l, p: (kj[s], 0)),
            pl.BlockSpec((TK, ROPE_PAD), lambda s, qi, kj, f, l, p: (kj[s], 0)),
            pl.BlockSpec((N_META, KV_LORA), lambda s, qi, kj, f, l, p: (mblk, 0)),
            pl.BlockSpec((N_META, ROPE_PAD), lambda s, qi, kj, f, l, p: (mblk, 0)),
            pl.BlockSpec(memory_space=pl.ANY),
        ],
        out_specs=pl.BlockSpec((TQ, A_HEADS * KV_LORA), lambda s, qi, kj, f, l, p: (qi[s], 0)),
        scratch_shapes=[pltpu.VMEM((rows, KV_LORA), BF16), pltpu.VMEM((rows, ROPE_PAD), BF16),
                        pltpu.VMEM((rows, 1), F32), pltpu.VMEM((rows, 1), F32),
                        pltpu.VMEM((rows, KV_LORA), F32)],
    )
    return pl.pallas_call(
        _prompt_attn_kernel,
        grid_spec=grid_spec,
        out_shape=jax.ShapeDtypeStruct(o_in.shape, BF16),
        input_output_aliases={5 + 6: 0},
        compiler_params=_cparams(("arbitrary",)),
        name="prompt_attn",
    )(*sched, ql, qr, kl, kr, kl, kr, o_in)


def _paged_attn_kernel(pt_ref, ql_ref, qr_ref, nl_ref, nr_ref, *rest):
    del pt_ref
    np_ = PAGES_PER_STEP
    lat_refs, rope_refs = rest[:np_], rest[np_:2 * np_]
    oin_ref, o_ref, qls_ref, qrs_ref, m_ref, l_ref, acc_ref = rest[2 * np_:]
    del oin_ref
    c = pl.program_id(1)
    nt = (((1,), (1,)), ((), ()))

    @pl.when(c == 0)
    def _():
        qls_ref[...] = _stack_heads(ql_ref, KV_LORA)
        qrs_ref[...] = _stack_heads(qr_ref, ROPE_PAD)[:, :A_ROPE]
        m_ref[...] = jnp.full(m_ref.shape, -jnp.inf, F32)
        l_ref[...] = jnp.zeros(l_ref.shape, F32)
        acc_ref[...] = jnp.zeros(acc_ref.shape, F32)

    def update(s, vals):
        m_old = m_ref[...]
        m_new = jnp.maximum(m_old, jnp.max(s, axis=1, keepdims=True))
        alpha = jnp.exp(m_old - m_new)
        p = jnp.exp(s - m_new)
        m_ref[...] = m_new
        l_ref[...] = alpha * l_ref[...] + jnp.sum(p, axis=1, keepdims=True)
        acc_ref[...] = alpha * acc_ref[...] + jnp.dot(p.astype(BF16), vals, preferred_element_type=F32)

    kl = jnp.concatenate([r[...] for r in lat_refs], axis=0).astype(BF16)
    kr = jnp.concatenate([r[...] for r in rope_refs], axis=0).astype(BF16)
    s = (lax.dot_general(qls_ref[...], kl, nt, preferred_element_type=F32)
         + lax.dot_general(qrs_ref[...], kr, nt, preferred_element_type=F32)) * ATTN_SCALE
    update(s, kl)

    @pl.when(c == pl.num_programs(1) - 1)
    def _():
        nl = nl_ref[...]
        s2 = (lax.dot_general(qls_ref[...], nl, nt, preferred_element_type=F32)
              + lax.dot_general(qrs_ref[...], nr_ref[:, :A_ROPE], nt, preferred_element_type=F32)) * ATTN_SCALE
        rows = A_HEADS * DEC_SEQ
        tq = lax.broadcasted_iota(jnp.int32, (rows, 1), 0) & (DEC_SEQ - 1)
        tk = lax.broadcasted_iota(jnp.int32, (1, DEC_SEQ), 1)
        update(jnp.where(tk <= tq, s2, -jnp.inf), nl)
        o = acc_ref[...] / l_ref[...]
        for h in range(A_HEADS):
            o_ref[:, h * KV_LORA:(h + 1) * KV_LORA] = o[h * DEC_SEQ:(h + 1) * DEC_SEQ, :].astype(BF16)


def _paged_attn(page_table, ql, qr, kl, kr, cache_lat, cache_rope, o_in):
    np_ = PAGES_PER_STEP
    n_pages = PAST_LEN // PAGE_SIZE
    rb0 = ROW_S0 // DEC_SEQ
    rows = A_HEADS * DEC_SEQ

    def page_spec(width, p):
        return pl.BlockSpec((None, PAGE_SIZE, width), lambda b, c, pt: (pt[b, c * np_ + p], 0, 0))

    rowblk = lambda width: pl.BlockSpec((DEC_SEQ, width), lambda b, c, pt: (rb0 + b, 0))
    grid_spec = pltpu.PrefetchScalarGridSpec(
        num_scalar_prefetch=1,
        grid=(DEC_BATCH, n_pages // np_),
        in_specs=[rowblk(A_HEADS * KV_LORA), rowblk(A_HEADS * ROPE_PAD), rowblk(KV_LORA), rowblk(ROPE_PAD)]
                 + [page_spec(KV_LORA, p) for p in range(np_)]
                 + [page_spec(A_ROPE, p) for p in range(np_)]
                 + [pl.BlockSpec(memory_space=pl.ANY)],
        out_specs=rowblk(A_HEADS * KV_LORA),
        scratch_shapes=[pltpu.VMEM((rows, KV_LORA), BF16), pltpu.VMEM((rows, A_ROPE), BF16),
                        pltpu.VMEM((rows, 1), F32), pltpu.VMEM((rows, 1), F32),
                        pltpu.VMEM((rows, KV_LORA), F32)],
    )
    return pl.pallas_call(
        _paged_attn_kernel,
        grid_spec=grid_spec,
        out_shape=jax.ShapeDtypeStruct(o_in.shape, BF16),
        input_output_aliases={1 + 4 + 2 * np_: 0},
        compiler_params=_cparams(("arbitrary", "arbitrary")),
        name="paged_attn",
    )(page_table, ql, qr, kl, kr, *([cache_lat] * np_), *([cache_rope] * np_), o_in)


def _uvproj_kernel(o_ref, w_ref, y_ref):
    y_ref[...] = jnp.dot(o_ref[...], w_ref[...], preferred_element_type=F32).astype(BF16)


def _uvproj(o, wuv):
    return pl.pallas_call(
        _uvproj_kernel,
        grid=(ROWS // TM, A_HEADS),
        in_specs=[pl.BlockSpec((TM, KV_LORA), lambda i, h: (i, h)),
                  pl.BlockSpec((None, KV_LORA, A_VD), lambda i, h: (h, 0, 0))],
        out_specs=pl.BlockSpec((TM, A_VD), lambda i, h: (i, h)),
        out_shape=jax.ShapeDtypeStruct((ROWS, A_HEADS * A_VD), BF16),
        compiler_params=_cparams(("parallel", "arbitrary")),
        name="mla_uvproj",
    )(o, wuv)


def _merge_kernel(ym_ref, ya_ref, wm_ref, wa_ref, gm_ref, ga_ref, t_ref):
    y_m = jnp.dot(ym_ref[...], wm_ref[...], preferred_element_type=F32)
    y_a = jnp.dot(ya_ref[...], wa_ref[...], preferred_element_type=F32)
    t_ref[...] = (gm_ref[...] * y_m + ga_ref[...] * y_a).astype(BF16)


def _merge(ym, ya, wm, wa, og):
    tn = 1024
    nb = D_MODEL // tn
    return pl.pallas_call(
        _merge_kernel,
        grid=(ROWS // TM, nb),
        in_specs=[pl.BlockSpec((TM, D_MODEL), lambda i, j: (i, 0)),
                  pl.BlockSpec((TM, D_MODEL), lambda i, j: (i, 0)),
                  pl.BlockSpec((D_MODEL, tn), lambda i, j: (0, j)),
                  pl.BlockSpec((D_MODEL, tn), lambda i, j: (0, j)),
                  pl.BlockSpec((TM, tn), lambda i, j: (i, nb + j)),
                  pl.BlockSpec((TM, tn), lambda i, j: (i, 2 * nb + j))],
        out_specs=pl.BlockSpec((TM, tn), lambda i, j: (i, j)),
        out_shape=jax.ShapeDtypeStruct((ROWS, D_MODEL), BF16),
        compiler_params=_cparams(("parallel", "arbitrary")),
        name="branch_merge",
    )(ym, ya, wm, wa, og, og)


def _layer_norm(z, g, b):
    mu = jnp.mean(z, axis=-1, keepdims=True)
    d = z - mu
    var = jnp.mean(d * d, axis=-1, keepdims=True)
    return d * lax.rsqrt(var + LN_EPS) * g + b


def _outln_kernel(t_ref, w_ref, x_ref, g_ref, b_ref, xf_ref, xb_ref):
    mix = jnp.dot(t_ref[...], w_ref[...], preferred_element_type=F32)
    x1 = _layer_norm(ALPHA * x_ref[...] + mix, g_ref[...], b_ref[...])
    xf_ref[...] = x1
    xb_ref[...] = x1.astype(BF16)


def _outln(t, w, x, g, b):
    row = pl.BlockSpec((TM, D_MODEL), lambda i: (i, 0))
    vec = pl.BlockSpec((1, D_MODEL), lambda i: (0, 0))
    sd = jax.ShapeDtypeStruct
    return pl.pallas_call(
        _outln_kernel,
        grid=(ROWS // TM,),
        in_specs=[row, pl.BlockSpec((D_MODEL, D_MODEL), lambda i: (0, 0)), row, vec, vec],
        out_specs=[row, row],
        out_shape=[sd((ROWS, D_MODEL), F32), sd((ROWS, D_MODEL), BF16)],
        compiler_params=_cparams(("parallel",)),
        name="out_proj_ln",
    )(t, w, x, g, b)


def _mlp_kernel(xb_ref, wu_ref, wd_ref, xf_ref, g_ref, b_ref, y_ref, acc_ref):
    f = pl.program_id(1)
    h = jnp.dot(xb_ref[...], wu_ref[...], preferred_element_type=F32)
    h = jnp.square(jnp.maximum(h, 0.0)).astype(BF16)
    part = jnp.dot(h, wd_ref[...], preferred_element_type=F32)

    @pl.when(f == 0)
    def _():
        acc_ref[...] = part

    @pl.when(f > 0)
    def _():
        acc_ref[...] += part

    @pl.when(f == pl.num_programs(1) - 1)
    def _():
        y_ref[...] = _layer_norm(ALPHA * xf_ref[...] + acc_ref[...], g_ref[...], b_ref[...])


def _mlp(xb, wu, wd, xf, g, b):
    tf = 512
    row = pl.BlockSpec((TM, D_MODEL), lambda i, f: (i, 0))
    vec = pl.BlockSpec((1, D_MODEL), lambda i, f: (0, 0))
    return pl.pallas_call(
        _mlp_kernel,
        grid=(ROWS // TM, D_FF // tf),
        in_specs=[row, pl.BlockSpec((D_MODEL, tf), lambda i, f: (0, f)),
                  pl.BlockSpec((tf, D_MODEL), lambda i, f: (f, 0)), row, vec, vec],
        out_specs=row,
        out_shape=jax.ShapeDtypeStruct((ROWS, D_MODEL), F32),
        scratch_shapes=[pltpu.VMEM((TM, D_MODEL), F32)],
        compiler_params=_cparams(("parallel", "arbitrary")),
        name="mlp_ln",
    )(xb, wu, wd, xf, g, b)


def _rot_cols(w):
    half = A_ROPE // 2
    return jnp.concatenate([-w[..., half:], w[..., :half]], axis=-1)


def _pad_cols(w, n):
    return jnp.pad(w, [(0, 0)] * (w.ndim - 1) + [(0, n - w.shape[-1])])


def _rope_tables():
    half = A_ROPE // 2
    pos = jnp.concatenate([
        N_META + jnp.arange(SEQ, dtype=jnp.int32),
        jnp.tile(PAST_LEN + jnp.arange(DEC_SEQ, dtype=jnp.int32), DEC_BATCH),
        jnp.arange(N_META, dtype=jnp.int32),
        jnp.zeros((ROWS - ROW_M0 - N_META,), jnp.int32)])
    inv = ROPE_BASE ** (-jnp.arange(half, dtype=F32) / half)
    ang = pos.astype(F32)[:, None] * inv
    cos, sin = jnp.cos(ang), jnp.sin(ang)
    return (_pad_cols(jnp.concatenate([cos, cos], axis=-1), ROPE_PAD),
            _pad_cols(jnp.concatenate([sin, sin], axis=-1), ROPE_PAD))


def kernel(x_prompt, x_sample, cache_latent, cache_krope, state_C, state_n, state_m, page_table, meta, w_in, b_in, mh_g, q_norm_g, kv_norm_g, w_uq, w_uk, w_uv, w_br_m, w_br_a, w_out, ln1_g, ln1_b, w_up, w_down, ln2_g, ln2_b):
    wi, bi = w_in[0], b_in[0]
    x_all = jnp.concatenate([
        x_prompt[0], x_sample.reshape(DEC_BATCH * DEC_SEQ, D_MODEL), meta,
        jnp.zeros((ROWS - ROW_M0 - N_META, D_MODEL), F32)], axis=0)
    cos_t, sin_t = _rope_tables()

    ones = lambda n: jnp.ones((1, n), F32)
    k_scale = jnp.concatenate([jnp.ones((1, 1024), F32), jnp.full((1, 1024), M_DK ** -0.5, F32),
                               jnp.ones((1, 2048), F32)], axis=1)
    qkv = _matmul(x_all, wi[:, OFF_Q:OFF_O].astype(BF16), bi[None, OFF_Q:OFF_O], k_scale,
                  act=None, out_dtype=BF16, tn=1024, name="inproj_qkv")
    w_og = jnp.concatenate([wi[:, OFF_O:OFF_I], wi[:, OFF_G:]], axis=1).astype(BF16)
    b_og = jnp.concatenate([bi[OFF_O:OFF_I], bi[OFF_G:]])[None]
    og = _matmul(x_all, w_og, b_og, ones(w_og.shape[1]), act="sigmoid", out_dtype=F32, tn=1024,
                 name="inproj_gates")

    def tail_cols(a):
        kr = a[..., OFF_KR:OFF_G]
        return jnp.concatenate([
            a[..., OFF_AQ:OFF_KR], _pad_cols(kr, ROPE_PAD), _pad_cols(_rot_cols(kr), ROPE_PAD),
            _pad_cols(a[..., OFF_I:OFF_AQ], LANE)], axis=-1)

    qn, ckv_f, ckv_b, kr_f, kr_b, gates = _tail(
        x_all, tail_cols(wi).astype(BF16), tail_cols(bi)[None], q_norm_g, kv_norm_g, cos_t, sin_t)

    mhg = mh_g
    ym = jnp.zeros((ROWS, M_HEADS * M_DV), BF16)
    zc = jnp.zeros((1, M_HEADS, M_DK, M_DV), F32)
    zn = jnp.zeros((1, M_HEADS, M_DK), F32)
    zm = jnp.zeros((1, M_HEADS, 1), F32)
    ym, c_m, n_m, m_m = _mlstm(qkv, gates, og, mhg, zc, zn, zm, ym, length=N_META, nseq=1, nsteps=1,
                               row0=ROW_M0, carry=True, name="mlstm_meta")
    ym, p_c, p_n, p_m = _mlstm(qkv, gates, og, mhg, c_m, n_m, m_m, ym, length=M_CHUNK_P, nseq=1,
                               nsteps=SEQ // M_CHUNK_P, row0=0, carry=True, name="mlstm_prompt")
    ym, s_c, s_n, s_m = _mlstm(qkv, gates, og, mhg, state_C[0], state_n[0], state_m[0][..., None], ym,
                               length=DEC_SEQ, nseq=SEQ_PER_STEP, nsteps=DEC_BATCH // SEQ_PER_STEP,
                               row0=ROW_S0, carry=False, name="mlstm_sample")

    wq = w_uq[0].reshape(Q_LORA, A_HEADS, A_NOPE + A_ROPE).transpose(1, 0, 2)
    wq_r = wq[..., A_NOPE:]
    wq_p = jnp.concatenate([wq[..., :A_NOPE], _pad_cols(wq_r, ROPE_PAD), _pad_cols(_rot_cols(wq_r), ROPE_PAD)],
                           axis=-1).astype(BF16)
    wuk = w_uk[0].transpose(1, 2, 0).astype(BF16)
    wuv = w_uv[0].transpose(1, 0, 2).astype(BF16)
    q_lat, q_rope = _qproj(qn, wq_p, wuk, cos_t, sin_t)
    o_lat = jnp.zeros((ROWS, A_HEADS * KV_LORA), BF16)
    o_lat = _prompt_attn(q_lat, q_rope, ckv_b, kr_b, o_lat)
    o_lat = _paged_attn(page_table, q_lat, q_rope, ckv_b, kr_b, cache_latent[0], cache_krope[0], o_lat)
    ya = _uvproj(o_lat, wuv)

    t = _merge(ym, ya, w_br_m[0].astype(BF16), w_br_a[0].astype(BF16), og)
    x1f, x1b = _outln(t, w_out[0].astype(BF16), x_all, ln1_g, ln1_b)
    y = _mlp(x1b, w_up[0].astype(BF16), w_down[0].astype(BF16), x1f, ln2_g, ln2_b)

    def prompt_rows(a, width):
        return jnp.concatenate([a[ROW_M0:ROW_M0 + N_META, :width], a[:SEQ, :width]], axis=0)[None, None]

    def sample_rows(a, width):
        return a[ROW_S0:ROW_M0, :width].reshape(1, DEC_BATCH, DEC_SEQ, width)

    return (y[:SEQ][None], y[ROW_S0:ROW_M0].reshape(DEC_BATCH, DEC_SEQ, D_MODEL),
            prompt_rows(ckv_f, KV_LORA), prompt_rows(kr_f, A_ROPE),
            p_c[None], p_n[None], p_m[..., 0][None],
            sample_rows(ckv_f, KV_LORA), sample_rows(kr_f, A_ROPE),
            s_c[None], s_n[None], s_m[..., 0][None])
```

```python
import functools
import math

import numpy as np
import jax
import jax.numpy as jnp
from jax import lax
from jax.experimental import pallas as pl
from jax.experimental.pallas import tpu as pltpu

F32 = jnp.float32
BF16 = jnp.bfloat16

D_MODEL = 2048
SEQ = 8192
DEC_BATCH = 128
DEC_SEQ = 8
PAST_LEN = 16384
PAGE_SIZE = 128
N_META = 16
M_HEADS = 8
M_DK = 128
M_DV = 256
A_HEADS = 16
A_NOPE = 128
A_ROPE = 64
A_VD = 128
Q_LORA = 512
KV_LORA = 256
ROPE_BASE = 10000.0
ATTN_SCALE = (A_NOPE + A_ROPE) ** -0.5
D_FF = 4 * D_MODEL
ALPHA = 2.0 ** 0.25
LN_EPS = 1e-5
RMS_EPS = 1e-6

OFF_Q, OFF_K, OFF_V, OFF_O, OFF_I, OFF_F, OFF_AQ, OFF_CKV, OFF_KR, OFF_G = (
    0, 1024, 2048, 4096, 6144, 6152, 6160, 6672, 6928, 6992)
D_IN = 11088

ROW_S0 = SEQ
ROW_M0 = SEQ + DEC_BATCH * DEC_SEQ
ROWS = 9728
TM = 512
LANE = 128
ROPE_PAD = LANE

M_CHUNK_P = 256
SEQ_PER_STEP = 2
TQ = 128
TK = 512
PAGES_PER_STEP = 32
KEY_GROUPS = 4
VMEM_LIMIT = 56 * 1024 * 1024


def _cparams(sem):
    return pltpu.CompilerParams(dimension_semantics=sem, vmem_limit_bytes=VMEM_LIMIT)


def _mm_kernel(x_ref, w_ref, b_ref, s_ref, o_ref, xb_ref, *, act):
    @pl.when(pl.program_id(1) == 0)
    def _():
        xb_ref[...] = x_ref[...].astype(BF16)

    y = jnp.dot(xb_ref[...], w_ref[...], preferred_element_type=F32)
    y = (y + b_ref[...]) * s_ref[...]
    if act == "sigmoid":
        y = jax.nn.sigmoid(y)
    o_ref[...] = y.astype(o_ref.dtype)


def _matmul(x, w, b, s, *, act, out_dtype, tn, name):
    m, k = x.shape
    n = w.shape[1]
    return pl.pallas_call(
        functools.partial(_mm_kernel, act=act),
        grid=(m // TM, n // tn),
        in_specs=[
            pl.BlockSpec((TM, k), lambda i, j: (i, 0)),
            pl.BlockSpec((k, tn), lambda i, j: (0, j)),
            pl.BlockSpec((1, tn), lambda i, j: (0, j)),
            pl.BlockSpec((1, tn), lambda i, j: (0, j)),
        ],
        out_specs=pl.BlockSpec((TM, tn), lambda i, j: (i, j)),
        out_shape=jax.ShapeDtypeStruct((m, n), out_dtype),
        scratch_shapes=[pltpu.VMEM((TM, k), BF16)],
        compiler_params=_cparams(("parallel", "arbitrary")),
        name=name,
    )(x, w, b, s)


T_AQ, T_CKV, T_KR, T_KROT, T_G, T_END = 0, 512, 768, 896, 1024, 1152


def _tail_kernel(x_ref, w_ref, b_ref, qg_ref, kg_ref, cos_ref, sin_ref,
                 qn_ref, cf_ref, cb_ref, ct_ref, rf_ref, rb_ref, g_ref):
    xb = x_ref[...].astype(BF16)
    acc = jnp.dot(xb, w_ref[...], preferred_element_type=F32) + b_ref[...]
    aq = acc[:, T_AQ:T_CKV]
    qn = aq * lax.rsqrt(jnp.mean(aq * aq, axis=-1, keepdims=True) + RMS_EPS) * qg_ref[...]
    qn_ref[...] = qn.astype(BF16)
    ckv = acc[:, T_CKV:T_KR]
    c = ckv * lax.rsqrt(jnp.mean(ckv * ckv, axis=-1, keepdims=True) + RMS_EPS) * kg_ref[...]
    cf_ref[...] = c
    cb_ref[...] = c.astype(BF16)
    ct_ref[...] = c.T.astype(BF16)
    kr = acc[:, T_KR:T_KROT] * cos_ref[...] + acc[:, T_KROT:T_G] * sin_ref[...]
    rf_ref[...] = kr
    rb_ref[...] = kr.astype(BF16)
    g_ref[...] = acc[:, T_G:T_END]


def _tail(x, w, b, qg, kg, cos, sin):
    row = lambda n: pl.BlockSpec((TM, n), lambda i: (i, 0))
    full = lambda a: pl.BlockSpec(a.shape, lambda i: (0,) * a.ndim)
    sd = jax.ShapeDtypeStruct
    return pl.pallas_call(
        _tail_kernel,
        grid=(ROWS // TM,),
        in_specs=[row(D_MODEL), full(w), full(b), full(qg), full(kg), row(ROPE_PAD), row(ROPE_PAD)],
        out_specs=[row(Q_LORA), row(KV_LORA), row(KV_LORA), pl.BlockSpec((KV_LORA, TM), lambda i: (0, i)),
                   row(ROPE_PAD), row(ROPE_PAD), row(LANE)],
        out_shape=[sd((ROWS, Q_LORA), BF16), sd((ROWS, KV_LORA), F32), sd((ROWS, KV_LORA), BF16),
                   sd((KV_LORA, ROWS), BF16),
                   sd((ROWS, ROPE_PAD), F32), sd((ROWS, ROPE_PAD), BF16), sd((ROWS, LANE), F32)],
        compiler_params=_cparams(("parallel",)),
        name="inproj_tail",
    )(x, w, b, qg, kg, cos, sin)


def _log_sigmoid(x):
    return jnp.minimum(x, 0.0) - jnp.log1p(jnp.exp(-jnp.abs(x)))


def _mlstm_head(q, k, v, ig_c, ig_r, lf_c, lf_r, c0, n0, m0, length):
    ti = lax.broadcasted_iota(jnp.int32, (length, length), 0)
    si = lax.broadcasted_iota(jnp.int32, (length, length), 1)
    causal = si <= ti
    b_c = jnp.sum(jnp.where(causal, lf_r, 0.0), axis=1, keepdims=True)
    b_r = jnp.sum(jnp.where(ti <= si, lf_c, 0.0), axis=0, keepdims=True)
    dlog = jnp.where(causal, b_c - b_r + ig_r, -jnp.inf)
    m = jnp.maximum(b_c + m0, jnp.max(dlog, axis=1, keepdims=True))
    w = jnp.exp(dlog - m)
    inter = jnp.exp(b_c + m0 - m)
    s = lax.dot_general(q, k, (((1,), (1,)), ((), ())), preferred_element_type=F32) * w
    qc = jnp.dot(q, c0.astype(BF16), preferred_element_type=F32)
    num = jnp.dot(s.astype(BF16), v, preferred_element_type=F32) + inter * qc
    qn = jnp.sum(q.astype(F32) * n0, axis=1, keepdims=True)
    nq = jnp.sum(s, axis=1, keepdims=True) + inter * qn
    h = num / jnp.maximum(jnp.abs(nq), jnp.exp(-m))
    m_end = m[length - 1:length, :]
    b_end = b_c[length - 1:length, :]
    w_end = jnp.exp(b_end - b_c + ig_c - m_end)
    dec = jnp.exp(b_end + m0 - m_end)
    kw = k.astype(F32) * w_end
    c = dec * c0 + lax.dot_general(kw.astype(BF16), v, (((0,), (0,)), ((), ())),
                                   preferred_element_type=F32)
    n = dec * n0 + jnp.sum(kw, axis=0, keepdims=True)
    return h, c, n, m_end


def _mlstm_kernel(q_ref, k_ref, v_ref, g_ref, og_ref, mhg_ref, c0_ref, n0_ref, m0_ref, yin_ref,
                  y_ref, c_ref, n_ref, m_ref, cs_ref, ns_ref, ms_ref, *, length, nseq, carry):
    del yin_ref
    step = pl.program_id(0)
    if carry:
        @pl.when(step == 0)
        def _():
            cs_ref[...] = c0_ref[...]
            ns_ref[...] = n0_ref[...]
            ms_ref[...] = m0_ref[...]
        c_src, n_src, m_src = cs_ref, ns_ref, ms_ref
    else:
        c_src, n_src, m_src = c0_ref, n0_ref, m0_ref

    for b in range(nseq):
        r0 = b * length
        g = g_ref[r0:r0 + length, :]
        gt = g.T
        lfc_all = _log_sigmoid(g)
        lfr_all = _log_sigmoid(gt)
        for h in range(M_HEADS):
            q = q_ref[r0:r0 + length, h * M_DK:(h + 1) * M_DK]
            k = k_ref[r0:r0 + length, h * M_DK:(h + 1) * M_DK]
            v = v_ref[r0:r0 + length, h * M_DV:(h + 1) * M_DV]
            hh, c, n, m_end = _mlstm_head(
                q, k, v,
                g[:, h:h + 1], gt[h:h + 1, :],
                lfc_all[:, M_HEADS + h:M_HEADS + h + 1], lfr_all[M_HEADS + h:M_HEADS + h + 1, :],
                c_src[b, h], n_src[b, h:h + 1, :], m_src[b, h:h + 1, :], length)
            mu = jnp.mean(hh, axis=1, keepdims=True)
            d = hh - mu
            var = jnp.mean(d * d, axis=1, keepdims=True)
            hn = d * lax.rsqrt(var + LN_EPS)
            cols = slice(h * M_DV, (h + 1) * M_DV)
            y_ref[r0:r0 + length, cols] = (hn * mhg_ref[:, cols] * og_ref[r0:r0 + length, cols]).astype(BF16)
            if carry:
                cs_ref[b, h] = c
                ns_ref[b, h:h + 1, :] = n
                ms_ref[b, h:h + 1, :] = m_end
            else:
                c_ref[b, h] = c
                n_ref[b, h:h + 1, :] = n
                m_ref[b, h:h + 1, :] = m_end

    if carry:
        @pl.when(step == pl.num_programs(0) - 1)
        def _():
            c_ref[...] = cs_ref[...]
            n_ref[...] = ns_ref[...]
            m_ref[...] = ms_ref[...]


def _mlstm(qkv, gates, og, mhg, c0, n0, m0, y_in, *, length, nseq, nsteps, row0, carry, name):
    rows = nseq * length
    rb0 = row0 // rows
    nstate = c0.shape[0]
    if carry:
        st = lambda *tail: pl.BlockSpec((nstate,) + tail, lambda s: (0,) * (1 + len(tail)))
    else:
        st = lambda *tail: pl.BlockSpec((nseq,) + tail, lambda s: (s,) + (0,) * len(tail))
    sd = jax.ShapeDtypeStruct
    return pl.pallas_call(
        functools.partial(_mlstm_kernel, length=length, nseq=nseq, carry=carry),
        grid=(nsteps,),
        in_specs=[
            pl.BlockSpec((rows, M_HEADS * M_DK), lambda s: (rb0 + s, 0)),
            pl.BlockSpec((rows, M_HEADS * M_DK), lambda s: (rb0 + s, 1)),
            pl.BlockSpec((rows, M_HEADS * M_DV), lambda s: (rb0 + s, 1)),
            pl.BlockSpec((rows, LANE), lambda s: (rb0 + s, 0)),
            pl.BlockSpec((rows, M_HEADS * M_DV), lambda s: (rb0 + s, 0)),
            pl.BlockSpec((1, M_HEADS * M_DV), lambda s: (0, 0)),
            st(M_HEADS, M_DK, M_DV), st(M_HEADS, M_DK), st(M_HEADS, 1),
            pl.BlockSpec(memory_space=pl.ANY),
        ],
        out_specs=[
            pl.BlockSpec((rows, M_HEADS * M_DV), lambda s: (rb0 + s, 0)),
            st(M_HEADS, M_DK, M_DV), st(M_HEADS, M_DK), st(M_HEADS, 1),
        ],
        out_shape=[sd(y_in.shape, BF16), sd(c0.shape, F32), sd(n0.shape, F32), sd(m0.shape, F32)],
        scratch_shapes=[pltpu.VMEM((nstate, M_HEADS, M_DK, M_DV), F32),
                        pltpu.VMEM((nstate, M_HEADS, M_DK), F32),
                        pltpu.VMEM((nstate, M_HEADS, 1), F32)] if carry else
                       [pltpu.VMEM((8, LANE), F32)] * 3,
        input_output_aliases={9: 0},
        compiler_params=_cparams(("arbitrary",)),
        name=name,
    )(qkv, qkv, qkv, gates, og, mhg, c0, n0, m0, y_in)


QW = A_NOPE + 2 * ROPE_PAD


def _qproj_kernel(qn_ref, wq_ref, wuk_ref, cos_ref, sin_ref, ql_ref, qr_ref):
    qn = qn_ref[...]
    for h in range(A_HEADS):
        q = jnp.dot(qn, wq_ref[h], preferred_element_type=F32)
        ql = jnp.dot(q[:, :A_NOPE].astype(BF16), wuk_ref[h], preferred_element_type=F32)
        ql_ref[:, h * KV_LORA:(h + 1) * KV_LORA] = ql.astype(BF16)
        qr = q[:, A_NOPE:A_NOPE + ROPE_PAD] * cos_ref[...] + q[:, A_NOPE + ROPE_PAD:] * sin_ref[...]
        qr_ref[:, h * ROPE_PAD:(h + 1) * ROPE_PAD] = qr.astype(BF16)


def _qproj(qn, wq, wuk, cos, sin):
    sd = jax.ShapeDtypeStruct
    row = lambda n: pl.BlockSpec((TM, n), lambda i: (i, 0))
    full = lambda a: pl.BlockSpec(a.shape, lambda i: (0,) * a.ndim)
    return pl.pallas_call(
        _qproj_kernel,
        grid=(ROWS // TM,),
        in_specs=[row(Q_LORA), full(wq), full(wuk), row(ROPE_PAD), row(ROPE_PAD)],
        out_specs=[row(A_HEADS * KV_LORA), row(A_HEADS * ROPE_PAD)],
        out_shape=[sd((ROWS, A_HEADS * KV_LORA), BF16), sd((ROWS, A_HEADS * ROPE_PAD), BF16)],
        compiler_params=_cparams(("parallel",)),
        name="mla_qproj",
    )(qn, wq, wuk, cos, sin)


def _stack_heads(ref, width):
    return jnp.concatenate([ref[:, h * width:(h + 1) * width] for h in range(A_HEADS)], axis=0)


NT_DIMS = (((1,), (1,)), ((), ()))
C_LOG2 = ATTN_SCALE * math.log2(math.e)
HEADS_PER_SUB = 2
QB = HEADS_PER_SUB * TQ
NSUB = A_HEADS // HEADS_PER_SUB


def _pair_rows(ref, g, width):
    h0 = g * HEADS_PER_SUB
    return jnp.concatenate([ref[:, (h0 + j) * width:(h0 + j + 1) * width] for j in range(HEADS_PER_SUB)],
                           axis=0)


def _softmax_update_t(st, vt, g, m_ref, l_ref, acc_ref, init):
    mx = jnp.max(st, axis=0, keepdims=True)
    if init:
        m_new = mx
        p = jnp.exp2(st - m_new)
        l_ref[g] = jnp.sum(p, axis=0, keepdims=True)
        acc_ref[g] = jnp.dot(vt, p.astype(BF16), preferred_element_type=F32)
    else:
        m_old = m_ref[g]
        m_new = jnp.maximum(m_old, mx)
        alpha = jnp.exp2(m_old - m_new)
        p = jnp.exp2(st - m_new)
        l_ref[g] = alpha * l_ref[g] + jnp.sum(p, axis=0, keepdims=True)
        acc_ref[g] = alpha * acc_ref[g] + jnp.dot(vt, p.astype(BF16), preferred_element_type=F32)
    m_ref[g] = m_new


def _prompt_attn_kernel(qi_ref, kj_ref, first_ref, last_ref, qpos_ref,
                        ql_ref, qr_ref, kl_ref, kr_ref, vt_ref, ml_ref, mr_ref, mvt_ref, wuv_ref, yin_ref,
                        y_ref, m_ref, l_ref, acc_ref):
    del yin_ref
    s_idx = pl.program_id(0)
    qpos = qpos_ref[s_idx] + (lax.broadcasted_iota(jnp.int32, (1, QB), 1) & (TQ - 1))

    def scores(klat, krope, g):
        return (lax.dot_general(klat, _pair_rows(ql_ref, g, KV_LORA), NT_DIMS, preferred_element_type=F32)
                + lax.dot_general(krope, _pair_rows(qr_ref, g, ROPE_PAD), NT_DIMS,
                                  preferred_element_type=F32)) * C_LOG2

    @pl.when(first_ref[s_idx] == 1)
    def _():
        kpos = lax.broadcasted_iota(jnp.int32, (N_META, 1), 0)
        for g in range(NSUB):
            st = jnp.where(kpos <= qpos, scores(ml_ref[...], mr_ref[...], g), -jnp.inf)
            _softmax_update_t(st, mvt_ref[...], g, m_ref, l_ref, acc_ref, init=True)

    def key_block(masked):
        kpos = N_META + kj_ref[s_idx] * TK + lax.broadcasted_iota(jnp.int32, (TK, 1), 0)
        st_next = scores(kl_ref[...], kr_ref[...], 0)
        for g in range(NSUB):
            st = st_next
            if g + 1 < NSUB:
                st_next = scores(kl_ref[...], kr_ref[...], g + 1)
            if masked:
                st = jnp.where(kpos <= qpos, st, -jnp.inf)
            _softmax_update_t(st, vt_ref[...], g, m_ref, l_ref, acc_ref, init=False)

    pl.when(last_ref[s_idx] == 0)(functools.partial(key_block, False))
    pl.when(last_ref[s_idx] == 1)(functools.partial(key_block, True))

    @pl.when(last_ref[s_idx] == 1)
    def _():
        for g in range(NSUB):
            o = (acc_ref[g] / l_ref[g]).T.astype(BF16)
            for j in range(HEADS_PER_SUB):
                h = g * HEADS_PER_SUB + j
                y = jnp.dot(o[j * TQ:(j + 1) * TQ], wuv_ref[h], preferred_element_type=F32)
                y_ref[:, h * A_VD:(h + 1) * A_VD] = y.astype(BF16)


def _prompt_schedule():
    qi, kj, first, last, qpos = [], [], [], [], []
    for i in range(SEQ // TQ):
        nk = (i * TQ + TQ - 1) // TK + 1
        for j in range(nk):
            qi.append(i); kj.append(j); first.append(int(j == 0)); last.append(int(j == nk - 1))
            qpos.append(N_META + i * TQ)
    qi.append(ROW_M0 // TQ); kj.append(0); first.append(1); last.append(1); qpos.append(0)
    return [jnp.asarray(np.asarray(a, np.int32)) for a in (qi, kj, first, last, qpos)]


def _prompt_attn(ql, qr, kl, kr, vt, mvt, wuv, y_in):
    sched = _prompt_schedule()
    nsteps = int(sched[0].shape[0])
    mblk = ROW_M0 // N_META
    const = lambda a: pl.BlockSpec(a.shape, lambda s, qi, kj, f, l, p: (0,) * a.ndim)
    grid_spec = pltpu.PrefetchScalarGridSpec(
        num_scalar_prefetch=5,
        grid=(nsteps,),
        in_specs=[
            pl.BlockSpec((TQ, A_HEADS * KV_LORA), lambda s, qi, kj, f, l, p: (qi[s], 0)),
            pl.BlockSpec((TQ, A_HEADS * ROPE_PAD), lambda s, qi, kj, f, l, p: (qi[s], 0)),
            pl.BlockSpec((TK, KV_LORA), lambda s, qi, kj, f, l, p: (kj[s], 0)),
            pl.BlockSpec((TK, ROPE_PAD), lambda s, qi, kj, f, l, p: (kj[s], 0)),
            pl.BlockSpec((KV_LORA, TK), lambda s, qi, kj, f, l, p: (0, kj[s])),
            pl.BlockSpec((N_META, KV_LORA), lambda s, qi, kj, f, l, p: (mblk, 0)),
            pl.BlockSpec((N_META, ROPE_PAD), lambda s, qi, kj, f, l, p: (mblk, 0)),
            const(mvt), const(wuv),
            pl.BlockSpec(memory_space=pl.ANY),
        ],
        out_specs=pl.BlockSpec((TQ, A_HEADS * A_VD), lambda s, qi, kj, f, l, p: (qi[s], 0)),
        scratch_shapes=[pltpu.VMEM((NSUB, 1, QB), F32), pltpu.VMEM((NSUB, 1, QB), F32),
                        pltpu.VMEM((NSUB, KV_LORA, QB), F32)],
    )
    return pl.pallas_call(
        _prompt_attn_kernel,
        grid_spec=grid_spec,
        out_shape=jax.ShapeDtypeStruct(y_in.shape, BF16),
        input_output_aliases={5 + 9: 0},
        compiler_params=_cparams(("arbitrary",)),
        name="prompt_attn",
    )(*sched, ql, qr, kl, kr, vt, kl, kr, mvt, wuv, y_in)


def _paged_attn_kernel(pt_ref, ql_ref, qr_ref, nl_ref, nr_ref, *rest):
    del pt_ref
    np_ = PAGES_PER_STEP
    lat_refs, rope_refs = rest[:np_], rest[np_:2 * np_]
    o_ref, qls_ref, qrs_ref, m_ref, l_ref, acc_ref = rest[2 * np_:]
    c = pl.program_id(1)
    ppg = np_ // KEY_GROUPS

    @pl.when(c == 0)
    def _():
        qls_ref[...] = _stack_heads(ql_ref, KV_LORA)
        qrs_ref[...] = _stack_heads(qr_ref, ROPE_PAD)[:, :A_ROPE]
        m_ref[...] = jnp.full(m_ref.shape, -jnp.inf, F32)
        l_ref[...] = jnp.zeros(l_ref.shape, F32)
        acc_ref[...] = jnp.zeros(acc_ref.shape, F32)

    def update(g, s, vals):
        m_old = m_ref[g]
        m_new = jnp.maximum(m_old, jnp.max(s, axis=1, keepdims=True))
        alpha = jnp.exp2(m_old - m_new)
        p = jnp.exp2(s - m_new)
        m_ref[g] = m_new
        l_ref[g] = alpha * l_ref[g] + jnp.sum(p, axis=1, keepdims=True)
        acc_ref[g] = alpha * acc_ref[g] + jnp.dot(p.astype(BF16), vals, preferred_element_type=F32)

    def group_scores(g):
        pages = range(g * ppg, (g + 1) * ppg)
        kl = jnp.concatenate([lat_refs[p][...] for p in pages], axis=0).astype(BF16)
        krt = jnp.concatenate([rope_refs[p][...] for p in pages], axis=1).astype(BF16)
        s = (lax.dot_general(qls_ref[...], kl, NT_DIMS, preferred_element_type=F32)
             + jnp.dot(qrs_ref[...], krt, preferred_element_type=F32)) * C_LOG2
        return s, kl

    nxt = group_scores(0)
    for g in range(KEY_GROUPS):
        s, kl = nxt
        if g + 1 < KEY_GROUPS:
            nxt = group_scores(g + 1)
        update(g, s, kl)

    @pl.when(c == pl.num_programs(1) - 1)
    def _():
        nl = nl_ref[...]
        s2 = (lax.dot_general(qls_ref[...], nl, NT_DIMS, preferred_element_type=F32)
              + lax.dot_general(qrs_ref[...], nr_ref[:, :A_ROPE], NT_DIMS,
                                preferred_element_type=F32)) * C_LOG2
        rows = A_HEADS * DEC_SEQ
        tq = lax.broadcasted_iota(jnp.int32, (rows, 1), 0) & (DEC_SEQ - 1)
        tk = lax.broadcasted_iota(jnp.int32, (1, DEC_SEQ), 1)
        update(0, jnp.where(tk <= tq, s2, -jnp.inf), nl)
        m_tot = m_ref[0]
        for g in range(1, KEY_GROUPS):
            m_tot = jnp.maximum(m_tot, m_ref[g])
        l_tot = jnp.zeros((rows, 1), F32)
        acc = jnp.zeros((rows, KV_LORA), F32)
        for g in range(KEY_GROUPS):
            w = jnp.exp2(m_ref[g] - m_tot)
            l_tot = l_tot + w * l_ref[g]
            acc = acc + w * acc_ref[g]
        o = acc / l_tot
        for h in range(A_HEADS):
            o_ref[:, h * KV_LORA:(h + 1) * KV_LORA] = o[h * DEC_SEQ:(h + 1) * DEC_SEQ, :].astype(BF16)


def _paged_attn(page_table, ql, qr, kl, kr, cache_lat, cache_rope_t):
    np_ = PAGES_PER_STEP
    n_pages = PAST_LEN // PAGE_SIZE
    rb0 = ROW_S0 // DEC_SEQ
    rows = A_HEADS * DEC_SEQ

    def page_spec(shape, p):
        return pl.BlockSpec((None,) + shape, lambda b, c, pt: (pt[b, c * np_ + p], 0, 0))

    rowblk = lambda width: pl.BlockSpec((DEC_SEQ, width), lambda b, c, pt: (rb0 + b, 0))
    grid_spec = pltpu.PrefetchScalarGridSpec(
        num_scalar_prefetch=1,
        grid=(DEC_BATCH, n_pages // np_),
        in_specs=[rowblk(A_HEADS * KV_LORA), rowblk(A_HEADS * ROPE_PAD), rowblk(KV_LORA), rowblk(ROPE_PAD)]
                 + [page_spec((PAGE_SIZE, KV_LORA), p) for p in range(np_)]
                 + [page_spec((A_ROPE, PAGE_SIZE), p) for p in range(np_)],
        out_specs=pl.BlockSpec((DEC_SEQ, A_HEADS * KV_LORA), lambda b, c, pt: (b, 0)),
        scratch_shapes=[pltpu.VMEM((rows, KV_LORA), BF16), pltpu.VMEM((rows, A_ROPE), BF16),
                        pltpu.VMEM((KEY_GROUPS, rows, 1), F32), pltpu.VMEM((KEY_GROUPS, rows, 1), F32),
                        pltpu.VMEM((KEY_GROUPS, rows, KV_LORA), F32)],
    )
    return pl.pallas_call(
        _paged_attn_kernel,
        grid_spec=grid_spec,
        out_shape=jax.ShapeDtypeStruct((DEC_BATCH * DEC_SEQ, A_HEADS * KV_LORA), BF16),
        compiler_params=_cparams(("arbitrary", "arbitrary")),
        name="paged_attn",
    )(page_table, ql, qr, kl, kr, *([cache_lat] * np_), *([cache_rope_t] * np_))


def _uvproj_kernel(o_ref, w_ref, yin_ref, y_ref):
    del yin_ref
    for h in range(A_HEADS):
        y = jnp.dot(o_ref[:, h * KV_LORA:(h + 1) * KV_LORA], w_ref[h], preferred_element_type=F32)
        y_ref[:, h * A_VD:(h + 1) * A_VD] = y.astype(BF16)


def _uvproj_sample(o, wuv, y_in):
    rb0 = ROW_S0 // TM
    return pl.pallas_call(
        _uvproj_kernel,
        grid=(o.shape[0] // TM,),
        in_specs=[pl.BlockSpec((TM, A_HEADS * KV_LORA), lambda i: (i, 0)),
                  pl.BlockSpec(wuv.shape, lambda i: (0, 0, 0)),
                  pl.BlockSpec(memory_space=pl.ANY)],
        out_specs=pl.BlockSpec((TM, A_HEADS * A_VD), lambda i: (rb0 + i, 0)),
        out_shape=jax.ShapeDtypeStruct(y_in.shape, BF16),
        input_output_aliases={2: 0},
        compiler_params=_cparams(("parallel",)),
        name="mla_uvproj_sample",
    )(o, wuv, y_in)


def _merge_kernel(ym_ref, ya_ref, wm_ref, wa_ref, gm_ref, ga_ref, t_ref):
    y_m = jnp.dot(ym_ref[...], wm_ref[...], preferred_element_type=F32)
    y_a = jnp.dot(ya_ref[...], wa_ref[...], preferred_element_type=F32)
    t_ref[...] = (gm_ref[...] * y_m + ga_ref[...] * y_a).astype(BF16)


def _merge(ym, ya, wm, wa, og):
    tn = 1024
    nb = D_MODEL // tn
    return pl.pallas_call(
        _merge_kernel,
        grid=(ROWS // TM, nb),
        in_specs=[pl.BlockSpec((TM, D_MODEL), lambda i, j: (i, 0)),
                  pl.BlockSpec((TM, D_MODEL), lambda i, j: (i, 0)),
                  pl.BlockSpec((D_MODEL, tn), lambda i, j: (0, j)),
                  pl.BlockSpec((D_MODEL, tn), lambda i, j: (0, j)),
                  pl.BlockSpec((TM, tn), lambda i, j: (i, nb + j)),
                  pl.BlockSpec((TM, tn), lambda i, j: (i, 2 * nb + j))],
        out_specs=pl.BlockSpec((TM, tn), lambda i, j: (i, j)),
        out_shape=jax.ShapeDtypeStruct((ROWS, D_MODEL), BF16),
        compiler_params=_cparams(("parallel", "arbitrary")),
        name="branch_merge",
    )(ym, ya, wm, wa, og, og)


def _layer_norm(z, g, b):
    mu = jnp.mean(z, axis=-1, keepdims=True)
    d = z - mu
    var = jnp.mean(d * d, axis=-1, keepdims=True)
    return d * lax.rsqrt(var + LN_EPS) * g + b


def _outln_kernel(t_ref, w_ref, x_ref, g_ref, b_ref, xf_ref, xb_ref):
    mix = jnp.dot(t_ref[...], w_ref[...], preferred_element_type=F32)
    x1 = _layer_norm(ALPHA * x_ref[...] + mix, g_ref[...], b_ref[...])
    xf_ref[...] = x1
    xb_ref[...] = x1.astype(BF16)


def _outln(t, w, x, g, b):
    row = pl.BlockSpec((TM, D_MODEL), lambda i: (i, 0))
    vec = pl.BlockSpec((1, D_MODEL), lambda i: (0, 0))
    sd = jax.ShapeDtypeStruct
    return pl.pallas_call(
        _outln_kernel,
        grid=(ROWS // TM,),
        in_specs=[row, pl.BlockSpec((D_MODEL, D_MODEL), lambda i: (0, 0)), row, vec, vec],
        out_specs=[row, row],
        out_shape=[sd((ROWS, D_MODEL), F32), sd((ROWS, D_MODEL), BF16)],
        compiler_params=_cparams(("parallel",)),
        name="out_proj_ln",
    )(t, w, x, g, b)


def _mlp_kernel(xb_ref, wu_ref, wd_ref, xf_ref, g_ref, b_ref, y_ref, acc_ref):
    f = pl.program_id(1)
    h = jnp.dot(xb_ref[...], wu_ref[...], preferred_element_type=F32)
    h = jnp.square(jnp.maximum(h, 0.0)).astype(BF16)
    part = jnp.dot(h, wd_ref[...], preferred_element_type=F32)

    @pl.when(f == 0)
    def _():
        acc_ref[...] = part

    @pl.when(f > 0)
    def _():
        acc_ref[...] += part

    @pl.when(f == pl.num_programs(1) - 1)
    def _():
        y_ref[...] = _layer_norm(ALPHA * xf_ref[...] + acc_ref[...], g_ref[...], b_ref[...])


def _mlp(xb, wu, wd, xf, g, b):
    tf = 1024
    row = pl.BlockSpec((TM, D_MODEL), lambda i, f: (i, 0))
    vec = pl.BlockSpec((1, D_MODEL), lambda i, f: (0, 0))
    return pl.pallas_call(
        _mlp_kernel,
        grid=(ROWS // TM, D_FF // tf),
        in_specs=[row, pl.BlockSpec((D_MODEL, tf), lambda i, f: (0, f)),
                  pl.BlockSpec((tf, D_MODEL), lambda i, f: (f, 0)), row, vec, vec],
        out_specs=row,
        out_shape=jax.ShapeDtypeStruct((ROWS, D_MODEL), F32),
        scratch_shapes=[pltpu.VMEM((TM, D_MODEL), F32)],
        compiler_params=_cparams(("parallel", "arbitrary")),
        name="mlp_ln",
    )(xb, wu, wd, xf, g, b)


def _rot_cols(w):
    half = A_ROPE // 2
    return jnp.concatenate([-w[..., half:], w[..., :half]], axis=-1)


def _pad_cols(w, n):
    return jnp.pad(w, [(0, 0)] * (w.ndim - 1) + [(0, n - w.shape[-1])])


def _rope_tables():
    half = A_ROPE // 2
    pos = jnp.concatenate([
        N_META + jnp.arange(SEQ, dtype=jnp.int32),
        jnp.tile(PAST_LEN + jnp.arange(DEC_SEQ, dtype=jnp.int32), DEC_BATCH),
        jnp.arange(N_META, dtype=jnp.int32),
        jnp.zeros((ROWS - ROW_M0 - N_META,), jnp.int32)])
    inv = ROPE_BASE ** (-jnp.arange(half, dtype=F32) / half)
    ang = pos.astype(F32)[:, None] * inv
    cos, sin = jnp.cos(ang), jnp.sin(ang)
    return (_pad_cols(jnp.concatenate([cos, cos], axis=-1), ROPE_PAD),
            _pad_cols(jnp.concatenate([sin, sin], axis=-1), ROPE_PAD))


def kernel(x_prompt, x_sample, cache_latent, cache_krope, state_C, state_n, state_m, page_table, meta, w_in, b_in, mh_g, q_norm_g, kv_norm_g, w_uq, w_uk, w_uv, w_br_m, w_br_a, w_out, ln1_g, ln1_b, w_up, w_down, ln2_g, ln2_b):
    wi, bi = w_in[0], b_in[0]
    x_all = jnp.concatenate([
        x_prompt[0], x_sample.reshape(DEC_BATCH * DEC_SEQ, D_MODEL), meta,
        jnp.zeros((ROWS - ROW_M0 - N_META, D_MODEL), F32)], axis=0)
    cos_t, sin_t = _rope_tables()

    ones = lambda n: jnp.ones((1, n), F32)
    k_scale = jnp.concatenate([jnp.ones((1, 1024), F32), jnp.full((1, 1024), M_DK ** -0.5, F32),
                               jnp.ones((1, 2048), F32)], axis=1)
    qkv = _matmul(x_all, wi[:, OFF_Q:OFF_O].astype(BF16), bi[None, OFF_Q:OFF_O], k_scale,
                  act=None, out_dtype=BF16, tn=1024, name="inproj_qkv")
    w_og = jnp.concatenate([wi[:, OFF_O:OFF_I], wi[:, OFF_G:]], axis=1).astype(BF16)
    b_og = jnp.concatenate([bi[OFF_O:OFF_I], bi[OFF_G:]])[None]
    og = _matmul(x_all, w_og, b_og, ones(w_og.shape[1]), act="sigmoid", out_dtype=F32, tn=1024,
                 name="inproj_gates")

    def tail_cols(a):
        kr = a[..., OFF_KR:OFF_G]
        return jnp.concatenate([
            a[..., OFF_AQ:OFF_KR], _pad_cols(kr, ROPE_PAD), _pad_cols(_rot_cols(kr), ROPE_PAD),
            _pad_cols(a[..., OFF_I:OFF_AQ], LANE)], axis=-1)

    qn, ckv_f, ckv_b, ckv_t, kr_f, kr_b, gates = _tail(
        x_all, tail_cols(wi).astype(BF16), tail_cols(bi)[None], q_norm_g, kv_norm_g, cos_t, sin_t)

    mhg = mh_g
    ym = jnp.zeros((ROWS, M_HEADS * M_DV), BF16)
    zc = jnp.zeros((1, M_HEADS, M_DK, M_DV), F32)
    zn = jnp.zeros((1, M_HEADS, M_DK), F32)
    zm = jnp.zeros((1, M_HEADS, 1), F32)
    ym, c_m, n_m, m_m = _mlstm(qkv, gates, og, mhg, zc, zn, zm, ym, length=N_META, nseq=1, nsteps=1,
                               row0=ROW_M0, carry=True, name="mlstm_meta")
    ym, p_c, p_n, p_m = _mlstm(qkv, gates, og, mhg, c_m, n_m, m_m, ym, length=M_CHUNK_P, nseq=1,
                               nsteps=SEQ // M_CHUNK_P, row0=0, carry=True, name="mlstm_prompt")
    ym, s_c, s_n, s_m = _mlstm(qkv, gates, og, mhg, state_C[0], state_n[0], state_m[0][..., None], ym,
                               length=DEC_SEQ, nseq=SEQ_PER_STEP, nsteps=DEC_BATCH // SEQ_PER_STEP,
                               row0=ROW_S0, carry=False, name="mlstm_sample")

    wq = w_uq[0].reshape(Q_LORA, A_HEADS, A_NOPE + A_ROPE).transpose(1, 0, 2)
    wq_r = wq[..., A_NOPE:]
    wq_p = jnp.concatenate([wq[..., :A_NOPE], _pad_cols(wq_r, ROPE_PAD), _pad_cols(_rot_cols(wq_r), ROPE_PAD)],
                           axis=-1).astype(BF16)
    wuk = w_uk[0].transpose(1, 2, 0).astype(BF16)
    wuv = w_uv[0].transpose(1, 0, 2).astype(BF16)
    q_lat, q_rope = _qproj(qn, wq_p, wuk, cos_t, sin_t)
    ya = jnp.zeros((ROWS, A_HEADS * A_VD), BF16)
    ya = _prompt_attn(q_lat, q_rope, ckv_b, kr_b, ckv_t, ckv_t[:, ROW_M0:ROW_M0 + N_META], wuv, ya)
    o_s = _paged_attn(page_table, q_lat, q_rope, ckv_b, kr_b, cache_latent[0],
                      jnp.swapaxes(cache_krope[0], 1, 2))
    ya = _uvproj_sample(o_s, wuv, ya)

    t = _merge(ym, ya, w_br_m[0].astype(BF16), w_br_a[0].astype(BF16), og)
    x1f, x1b = _outln(t, w_out[0].astype(BF16), x_all, ln1_g, ln1_b)
    y = _mlp(x1b, w_up[0].astype(BF16), w_down[0].astype(BF16), x1f, ln2_g, ln2_b)

    def prompt_rows(a, width):
        return jnp.concatenate([a[ROW_M0:ROW_M0 + N_META, :width], a[:SEQ, :width]], axis=0)[None, None]

    def sample_rows(a, width):
        return a[ROW_S0:ROW_M0, :width].reshape(1, DEC_BATCH, DEC_SEQ, width)

    return (y[:SEQ][None], y[ROW_S0:ROW_M0].reshape(DEC_BATCH, DEC_SEQ, D_MODEL),
            prompt_rows(ckv_f, KV_LORA), prompt_rows(kr_f, A_ROPE),
            p_c[None], p_n[None], p_m[..., 0][None],
            sample_rows(ckv_f, KV_LORA), sample_rows(kr_f, A_ROPE),
            s_c[None], s_n[None], s_m[..., 0][None])
```

```python
import functools
import math

import numpy as np
import jax
import jax.numpy as jnp
from jax import lax
from jax.experimental import pallas as pl
from jax.experimental.pallas import tpu as pltpu

F32 = jnp.float32
BF16 = jnp.bfloat16

D_MODEL = 2048
SEQ = 8192
DEC_BATCH = 128
DEC_SEQ = 8
PAST_LEN = 16384
PAGE_SIZE = 128
N_META = 16
M_HEADS = 8
M_DK = 128
M_DV = 256
A_HEADS = 16
A_NOPE = 128
A_ROPE = 64
A_VD = 128
Q_LORA = 512
KV_LORA = 256
ROPE_BASE = 10000.0
ATTN_SCALE = (A_NOPE + A_ROPE) ** -0.5
D_FF = 4 * D_MODEL
ALPHA = 2.0 ** 0.25
LN_EPS = 1e-5
RMS_EPS = 1e-6

OFF_Q, OFF_K, OFF_V, OFF_O, OFF_I, OFF_F, OFF_AQ, OFF_CKV, OFF_KR, OFF_G = (
    0, 1024, 2048, 4096, 6144, 6152, 6160, 6672, 6928, 6992)
D_IN = 11088

ROW_S0 = SEQ
ROW_M0 = SEQ + DEC_BATCH * DEC_SEQ
ROWS = 9728
TM = 512
LANE = 128
ROPE_PAD = LANE

M_CHUNK_P = 256
SEQ_PER_STEP = 2
TQ = 128
TK = 512
PAGES_PER_STEP = 16
KEY_GROUPS = 2
VMEM_LIMIT = 56 * 1024 * 1024


def _cparams(sem):
    return pltpu.CompilerParams(dimension_semantics=sem, vmem_limit_bytes=VMEM_LIMIT)


def _mm_kernel(x_ref, w_ref, b_ref, s_ref, o_ref, xb_ref, *, act):
    @pl.when(pl.program_id(1) == 0)
    def _():
        xb_ref[...] = x_ref[...].astype(BF16)

    y = jnp.dot(xb_ref[...], w_ref[...], preferred_element_type=F32)
    y = (y + b_ref[...]) * s_ref[...]
    if act == "sigmoid":
        y = jax.nn.sigmoid(y)
    o_ref[...] = y.astype(o_ref.dtype)


def _matmul(x, w, b, s, *, act, out_dtype, tn, name):
    m, k = x.shape
    n = w.shape[1]
    return pl.pallas_call(
        functools.partial(_mm_kernel, act=act),
        grid=(m // TM, n // tn),
        in_specs=[
            pl.BlockSpec((TM, k), lambda i, j: (i, 0)),
            pl.BlockSpec((k, tn), lambda i, j: (0, j)),
            pl.BlockSpec((1, tn), lambda i, j: (0, j)),
            pl.BlockSpec((1, tn), lambda i, j: (0, j)),
        ],
        out_specs=pl.BlockSpec((TM, tn), lambda i, j: (i, j)),
        out_shape=jax.ShapeDtypeStruct((m, n), out_dtype),
        scratch_shapes=[pltpu.VMEM((TM, k), BF16)],
        compiler_params=_cparams(("parallel", "arbitrary")),
        name=name,
    )(x, w, b, s)


T_AQ, T_CKV, T_KR, T_KROT, T_G, T_END = 0, 512, 768, 896, 1024, 1152


def _tail_kernel(x_ref, w_ref, b_ref, qg_ref, kg_ref, cos_ref, sin_ref,
                 qn_ref, cf_ref, cb_ref, ct_ref, rf_ref, rb_ref, g_ref):
    xb = x_ref[...].astype(BF16)
    acc = jnp.dot(xb, w_ref[...], preferred_element_type=F32) + b_ref[...]
    aq = acc[:, T_AQ:T_CKV]
    qn = aq * lax.rsqrt(jnp.mean(aq * aq, axis=-1, keepdims=True) + RMS_EPS) * qg_ref[...]
    qn_ref[...] = qn.astype(BF16)
    ckv = acc[:, T_CKV:T_KR]
    c = ckv * lax.rsqrt(jnp.mean(ckv * ckv, axis=-1, keepdims=True) + RMS_EPS) * kg_ref[...]
    cf_ref[...] = c
    cb_ref[...] = c.astype(BF16)
    ct_ref[...] = c.T.astype(BF16)
    kr = acc[:, T_KR:T_KROT] * cos_ref[...] + acc[:, T_KROT:T_G] * sin_ref[...]
    rf_ref[...] = kr
    rb_ref[...] = kr.astype(BF16)
    g_ref[...] = acc[:, T_G:T_END]


def _tail(x, w, b, qg, kg, cos, sin):
    row = lambda n: pl.BlockSpec((TM, n), lambda i: (i, 0))
    full = lambda a: pl.BlockSpec(a.shape, lambda i: (0,) * a.ndim)
    sd = jax.ShapeDtypeStruct
    return pl.pallas_call(
        _tail_kernel,
        grid=(ROWS // TM,),
        in_specs=[row(D_MODEL), full(w), full(b), full(qg), full(kg), row(ROPE_PAD), row(ROPE_PAD)],
        out_specs=[row(Q_LORA), row(KV_LORA), row(KV_LORA), pl.BlockSpec((KV_LORA, TM), lambda i: (0, i)),
                   row(ROPE_PAD), row(ROPE_PAD), row(LANE)],
        out_shape=[sd((ROWS, Q_LORA), BF16), sd((ROWS, KV_LORA), F32), sd((ROWS, KV_LORA), BF16),
                   sd((KV_LORA, ROWS), BF16),
                   sd((ROWS, ROPE_PAD), F32), sd((ROWS, ROPE_PAD), BF16), sd((ROWS, LANE), F32)],
        compiler_params=_cparams(("parallel",)),
        name="inproj_tail",
    )(x, w, b, qg, kg, cos, sin)


def _log_sigmoid(x):
    return jnp.minimum(x, 0.0) - jnp.log1p(jnp.exp(-jnp.abs(x)))


def _mlstm_head(q, k, v, ig_c, ig_r, lf_c, lf_r, c0, n0, m0, length):
    ti = lax.broadcasted_iota(jnp.int32, (length, length), 0)
    si = lax.broadcasted_iota(jnp.int32, (length, length), 1)
    causal = si <= ti
    b_c = jnp.sum(jnp.where(causal, lf_r, 0.0), axis=1, keepdims=True)
    b_r = jnp.sum(jnp.where(ti <= si, lf_c, 0.0), axis=0, keepdims=True)
    dlog = jnp.where(causal, b_c - b_r + ig_r, -jnp.inf)
    m = jnp.maximum(b_c + m0, jnp.max(dlog, axis=1, keepdims=True))
    w = jnp.exp(dlog - m)
    inter = jnp.exp(b_c + m0 - m)
    s = lax.dot_general(q, k, (((1,), (1,)), ((), ())), preferred_element_type=F32) * w
    qc = jnp.dot(q, c0.astype(BF16), preferred_element_type=F32)
    num = jnp.dot(s.astype(BF16), v, preferred_element_type=F32) + inter * qc
    qn = jnp.sum(q.astype(F32) * n0, axis=1, keepdims=True)
    nq = jnp.sum(s, axis=1, keepdims=True) + inter * qn
    h = num / jnp.maximum(jnp.abs(nq), jnp.exp(-m))
    m_end = m[length - 1:length, :]
    b_end = b_c[length - 1:length, :]
    w_end = jnp.exp(b_end - b_c + ig_c - m_end)
    dec = jnp.exp(b_end + m0 - m_end)
    kw = k.astype(F32) * w_end
    c = dec * c0 + lax.dot_general(kw.astype(BF16), v, (((0,), (0,)), ((), ())),
                                   preferred_element_type=F32)
    n = dec * n0 + jnp.sum(kw, axis=0, keepdims=True)
    return h, c, n, m_end


def _mlstm_kernel(q_ref, k_ref, v_ref, g_ref, og_ref, mhg_ref, c0_ref, n0_ref, m0_ref, yin_ref,
                  y_ref, c_ref, n_ref, m_ref, cs_ref, ns_ref, ms_ref, *, length, nseq, carry):
    del yin_ref
    step = pl.program_id(0)
    if carry:
        @pl.when(step == 0)
        def _():
            cs_ref[...] = c0_ref[...]
            ns_ref[...] = n0_ref[...]
            ms_ref[...] = m0_ref[...]
        c_src, n_src, m_src = cs_ref, ns_ref, ms_ref
    else:
        c_src, n_src, m_src = c0_ref, n0_ref, m0_ref

    for b in range(nseq):
        r0 = b * length
        g = g_ref[r0:r0 + length, :]
        gt = g.T
        lfc_all = _log_sigmoid(g)
        lfr_all = _log_sigmoid(gt)
        for h in range(M_HEADS):
            q = q_ref[r0:r0 + length, h * M_DK:(h + 1) * M_DK]
            k = k_ref[r0:r0 + length, h * M_DK:(h + 1) * M_DK]
            v = v_ref[r0:r0 + length, h * M_DV:(h + 1) * M_DV]
            hh, c, n, m_end = _mlstm_head(
                q, k, v,
                g[:, h:h + 1], gt[h:h + 1, :],
                lfc_all[:, M_HEADS + h:M_HEADS + h + 1], lfr_all[M_HEADS + h:M_HEADS + h + 1, :],
                c_src[b, h], n_src[b, h:h + 1, :], m_src[b, h:h + 1, :], length)
            mu = jnp.mean(hh, axis=1, keepdims=True)
            d = hh - mu
            var = jnp.mean(d * d, axis=1, keepdims=True)
            hn = d * lax.rsqrt(var + LN_EPS)
            cols = slice(h * M_DV, (h + 1) * M_DV)
            y_ref[r0:r0 + length, cols] = (hn * mhg_ref[:, cols] * og_ref[r0:r0 + length, cols]).astype(BF16)
            if carry:
                cs_ref[b, h] = c
                ns_ref[b, h:h + 1, :] = n
                ms_ref[b, h:h + 1, :] = m_end
            else:
                c_ref[b, h] = c
                n_ref[b, h:h + 1, :] = n
                m_ref[b, h:h + 1, :] = m_end

    if carry:
        @pl.when(step == pl.num_programs(0) - 1)
        def _():
            c_ref[...] = cs_ref[...]
            n_ref[...] = ns_ref[...]
            m_ref[...] = ms_ref[...]


def _mlstm(qkv, gates, og, mhg, c0, n0, m0, y_in, *, length, nseq, nsteps, row0, carry, name):
    rows = nseq * length
    rb0 = row0 // rows
    nstate = c0.shape[0]
    if carry:
        st = lambda *tail: pl.BlockSpec((nstate,) + tail, lambda s: (0,) * (1 + len(tail)))
    else:
        st = lambda *tail: pl.BlockSpec((nseq,) + tail, lambda s: (s,) + (0,) * len(tail))
    sd = jax.ShapeDtypeStruct
    return pl.pallas_call(
        functools.partial(_mlstm_kernel, length=length, nseq=nseq, carry=carry),
        grid=(nsteps,),
        in_specs=[
            pl.BlockSpec((rows, M_HEADS * M_DK), lambda s: (rb0 + s, 0)),
            pl.BlockSpec((rows, M_HEADS * M_DK), lambda s: (rb0 + s, 1)),
            pl.BlockSpec((rows, M_HEADS * M_DV), lambda s: (rb0 + s, 1)),
            pl.BlockSpec((rows, LANE), lambda s: (rb0 + s, 0)),
            pl.BlockSpec((rows, M_HEADS * M_DV), lambda s: (rb0 + s, 0)),
            pl.BlockSpec((1, M_HEADS * M_DV), lambda s: (0, 0)),
            st(M_HEADS, M_DK, M_DV), st(M_HEADS, M_DK), st(M_HEADS, 1),
            pl.BlockSpec(memory_space=pl.ANY),
        ],
        out_specs=[
            pl.BlockSpec((rows, M_HEADS * M_DV), lambda s: (rb0 + s, 0)),
            st(M_HEADS, M_DK, M_DV), st(M_HEADS, M_DK), st(M_HEADS, 1),
        ],
        out_shape=[sd(y_in.shape, BF16), sd(c0.shape, F32), sd(n0.shape, F32), sd(m0.shape, F32)],
        scratch_shapes=[pltpu.VMEM((nstate, M_HEADS, M_DK, M_DV), F32),
                        pltpu.VMEM((nstate, M_HEADS, M_DK), F32),
                        pltpu.VMEM((nstate, M_HEADS, 1), F32)] if carry else
                       [pltpu.VMEM((8, LANE), F32)] * 3,
        input_output_aliases={9: 0},
        compiler_params=_cparams(("arbitrary",)),
        name=name,
    )(qkv, qkv, qkv, gates, og, mhg, c0, n0, m0, y_in)


QW = A_NOPE + 2 * ROPE_PAD


def _qproj_kernel(qn_ref, wq_ref, wuk_ref, cos_ref, sin_ref, ql_ref, qr_ref):
    qn = qn_ref[...]
    for h in range(A_HEADS):
        q = jnp.dot(qn, wq_ref[h], preferred_element_type=F32)
        ql = jnp.dot(q[:, :A_NOPE].astype(BF16), wuk_ref[h], preferred_element_type=F32)
        ql_ref[:, h * KV_LORA:(h + 1) * KV_LORA] = ql.astype(BF16)
        qr = q[:, A_NOPE:A_NOPE + ROPE_PAD] * cos_ref[...] + q[:, A_NOPE + ROPE_PAD:] * sin_ref[...]
        qr_ref[:, h * ROPE_PAD:(h + 1) * ROPE_PAD] = qr.astype(BF16)


def _qproj(qn, wq, wuk, cos, sin):
    sd = jax.ShapeDtypeStruct
    row = lambda n: pl.BlockSpec((TM, n), lambda i: (i, 0))
    full = lambda a: pl.BlockSpec(a.shape, lambda i: (0,) * a.ndim)
    return pl.pallas_call(
        _qproj_kernel,
        grid=(ROWS // TM,),
        in_specs=[row(Q_LORA), full(wq), full(wuk), row(ROPE_PAD), row(ROPE_PAD)],
        out_specs=[row(A_HEADS * KV_LORA), row(A_HEADS * ROPE_PAD)],
        out_shape=[sd((ROWS, A_HEADS * KV_LORA), BF16), sd((ROWS, A_HEADS * ROPE_PAD), BF16)],
        compiler_params=_cparams(("parallel",)),
        name="mla_qproj",
    )(qn, wq, wuk, cos, sin)


def _stack_heads(ref, width):
    return jnp.concatenate([ref[:, h * width:(h + 1) * width] for h in range(A_HEADS)], axis=0)


NT_DIMS = (((1,), (1,)), ((), ()))
C_LOG2 = ATTN_SCALE * math.log2(math.e)
HEADS_PER_SUB = 2
QB = HEADS_PER_SUB * TQ
NSUB = A_HEADS // HEADS_PER_SUB
SCORE_LOOKAHEAD = 2


def _pair_rows(ref, g, width):
    h0 = g * HEADS_PER_SUB
    return jnp.concatenate([ref[:, (h0 + j) * width:(h0 + j + 1) * width] for j in range(HEADS_PER_SUB)],
                           axis=0)


def _softmax_update_t(st, vt, g, m_ref, l_ref, acc_ref, init):
    mx = jnp.max(st, axis=0, keepdims=True)
    if init:
        m_new = mx
        p = jnp.exp2(st - m_new)
        l_ref[g] = jnp.sum(p, axis=0, keepdims=True)
        acc_ref[g] = jnp.dot(vt, p.astype(BF16), preferred_element_type=F32)
    else:
        m_old = m_ref[g]
        m_new = jnp.maximum(m_old, mx)
        alpha = jnp.exp2(m_old - m_new)
        p = jnp.exp2(st - m_new)
        l_ref[g] = alpha * l_ref[g] + jnp.sum(p, axis=0, keepdims=True)
        acc_ref[g] = alpha * acc_ref[g] + jnp.dot(vt, p.astype(BF16), preferred_element_type=F32)
    m_ref[g] = m_new


def _prompt_attn_kernel(qi_ref, kj_ref, first_ref, last_ref, qpos_ref,
                        ql_ref, qr_ref, kl_ref, kr_ref, vt_ref, ml_ref, mr_ref, mvt_ref, wuv_ref, yin_ref,
                        y_ref, m_ref, l_ref, acc_ref):
    del yin_ref
    s_idx = pl.program_id(0)
    qpos = qpos_ref[s_idx] + (lax.broadcasted_iota(jnp.int32, (1, QB), 1) & (TQ - 1))

    def scores(klat, krope, g):
        return (lax.dot_general(klat, _pair_rows(ql_ref, g, KV_LORA), NT_DIMS, preferred_element_type=F32)
                + lax.dot_general(krope, _pair_rows(qr_ref, g, ROPE_PAD), NT_DIMS,
                                  preferred_element_type=F32)) * C_LOG2

    def sweep(k_lat_ref, k_rope_ref, v_t_ref, kpos, init):
        ahead = [scores(k_lat_ref[...], k_rope_ref[...], g) for g in range(SCORE_LOOKAHEAD)]
        for g in range(NSUB):
            st = ahead.pop(0)
            if g + SCORE_LOOKAHEAD < NSUB:
                ahead.append(scores(k_lat_ref[...], k_rope_ref[...], g + SCORE_LOOKAHEAD))
            if kpos is not None:
                st = jnp.where(kpos <= qpos, st, -jnp.inf)
            _softmax_update_t(st, v_t_ref[...], g, m_ref, l_ref, acc_ref, init=init)

    @pl.when(first_ref[s_idx] == 1)
    def _():
        kpos = lax.broadcasted_iota(jnp.int32, (N_META, 1), 0)
        sweep(ml_ref, mr_ref, mvt_ref, kpos, init=True)

    def key_block(masked):
        kpos = N_META + kj_ref[s_idx] * TK + lax.broadcasted_iota(jnp.int32, (TK, 1), 0)
        sweep(kl_ref, kr_ref, vt_ref, kpos if masked else None, init=False)

    pl.when(last_ref[s_idx] == 0)(functools.partial(key_block, False))
    pl.when(last_ref[s_idx] == 1)(functools.partial(key_block, True))

    @pl.when(last_ref[s_idx] == 1)
    def _():
        for g in range(NSUB):
            o = (acc_ref[g] / l_ref[g]).T.astype(BF16)
            for j in range(HEADS_PER_SUB):
                h = g * HEADS_PER_SUB + j
                y = jnp.dot(o[j * TQ:(j + 1) * TQ], wuv_ref[h], preferred_element_type=F32)
                y_ref[:, h * A_VD:(h + 1) * A_VD] = y.astype(BF16)


def _prompt_schedule():
    qi, kj, first, last, qpos = [], [], [], [], []
    for i in range(SEQ // TQ):
        nk = (i * TQ + TQ - 1) // TK + 1
        for j in range(nk):
            qi.append(i); kj.append(j); first.append(int(j == 0)); last.append(int(j == nk - 1))
            qpos.append(N_META + i * TQ)
    qi.append(ROW_M0 // TQ); kj.append(0); first.append(1); last.append(1); qpos.append(0)
    return [jnp.asarray(np.asarray(a, np.int32)) for a in (qi, kj, first, last, qpos)]


def _prompt_attn(ql, qr, kl, kr, vt, mvt, wuv, y_in):
    sched = _prompt_schedule()
    nsteps = int(sched[0].shape[0])
    mblk = ROW_M0 // N_META
    const = lambda a: pl.BlockSpec(a.shape, lambda s, qi, kj, f, l, p: (0,) * a.ndim)
    grid_spec = pltpu.PrefetchScalarGridSpec(
        num_scalar_prefetch=5,
        grid=(nsteps,),
        in_specs=[
            pl.BlockSpec((TQ, A_HEADS * KV_LORA), lambda s, qi, kj, f, l, p: (qi[s], 0)),
            pl.BlockSpec((TQ, A_HEADS * ROPE_PAD), lambda s, qi, kj, f, l, p: (qi[s], 0)),
            pl.BlockSpec((TK, KV_LORA), lambda s, qi, kj, f, l, p: (kj[s], 0)),
            pl.BlockSpec((TK, ROPE_PAD), lambda s, qi, kj, f, l, p: (kj[s], 0)),
            pl.BlockSpec((KV_LORA, TK), lambda s, qi, kj, f, l, p: (0, kj[s])),
            pl.BlockSpec((N_META, KV_LORA), lambda s, qi, kj, f, l, p: (mblk, 0)),
            pl.BlockSpec((N_META, ROPE_PAD), lambda s, qi, kj, f, l, p: (mblk, 0)),
            const(mvt), const(wuv),
            pl.BlockSpec(memory_space=pl.ANY),
        ],
        out_specs=pl.BlockSpec((TQ, A_HEADS * A_VD), lambda s, qi, kj, f, l, p: (qi[s], 0)),
        scratch_shapes=[pltpu.VMEM((NSUB, 1, QB), F32), pltpu.VMEM((NSUB, 1, QB), F32),
                        pltpu.VMEM((NSUB, KV_LORA, QB), F32)],
    )
    return pl.pallas_call(
        _prompt_attn_kernel,
        grid_spec=grid_spec,
        out_shape=jax.ShapeDtypeStruct(y_in.shape, BF16),
        input_output_aliases={5 + 9: 0},
        compiler_params=_cparams(("arbitrary",)),
        name="prompt_attn",
    )(*sched, ql, qr, kl, kr, vt, kl, kr, mvt, wuv, y_in)


def _paged_attn_kernel(pt_ref, ql_ref, qr_ref, nl_ref, nr_ref, lat_hbm, rope_hbm, o_ref,
                       lat_buf, rope_buf, sem, qls_ref, qrs_ref, m_ref, l_ref, acc_ref):
    b = pl.program_id(0)
    nseq = pl.num_programs(0)
    ppc = PAGES_PER_STEP
    nch = (PAST_LEN // PAGE_SIZE) // ppc
    ppg = ppc // KEY_GROUPS

    def page_copies(seq, chunk, slot):
        cps = []
        for p in range(ppc):
            page = pt_ref[seq, chunk * ppc + p]
            cps.append(pltpu.make_async_copy(lat_hbm.at[page], lat_buf.at[slot, p], sem.at[0, slot]))
            cps.append(pltpu.make_async_copy(rope_hbm.at[page], rope_buf.at[slot, p], sem.at[1, slot]))
        return cps

    def start(seq, chunk, slot):
        for cp in page_copies(seq, chunk, slot):
            cp.start()

    @pl.when(b == 0)
    def _():
        start(0, 0, 0)

    qls_ref[...] = _stack_heads(ql_ref, KV_LORA)
    qrs_ref[...] = _stack_heads(qr_ref, ROPE_PAD)[:, :A_ROPE]
    m_ref[...] = jnp.full(m_ref.shape, -jnp.inf, F32)
    l_ref[...] = jnp.zeros(l_ref.shape, F32)
    acc_ref[...] = jnp.zeros(acc_ref.shape, F32)

    def update(g, s, vals):
        m_old = m_ref[g]
        m_new = jnp.maximum(m_old, jnp.max(s, axis=1, keepdims=True))
        alpha = jnp.exp2(m_old - m_new)
        p = jnp.exp2(s - m_new)
        m_ref[g] = m_new
        l_ref[g] = alpha * l_ref[g] + jnp.sum(p, axis=1, keepdims=True)
        acc_ref[g] = alpha * acc_ref[g] + jnp.dot(p.astype(BF16), vals, preferred_element_type=F32)

    def group_scores(slot, g):
        pages = range(g * ppg, (g + 1) * ppg)
        kl = jnp.concatenate([lat_buf[slot, p] for p in pages], axis=0).astype(BF16)
        krt = jnp.concatenate([rope_buf[slot, p] for p in pages], axis=1).astype(BF16)
        s = (lax.dot_general(qls_ref[...], kl, NT_DIMS, preferred_element_type=F32)
             + jnp.dot(qrs_ref[...], krt, preferred_element_type=F32)) * C_LOG2
        return s, kl

    def consume(chunk, slot):
        for cp in page_copies(b, chunk, slot):
            cp.wait()
        nxt = group_scores(slot, 0)
        for g in range(KEY_GROUPS):
            s, kl = nxt
            if g + 1 < KEY_GROUPS:
                nxt = group_scores(slot, g + 1)
            update(g, s, kl)

    def chunk_pair(i, carry):
        start(b, 2 * i + 1, 1)
        consume(2 * i, 0)

        @pl.when(i < nch // 2 - 1)
        def _():
            start(b, 2 * i + 2, 0)

        @pl.when(jnp.logical_and(i == nch // 2 - 1, b + 1 < nseq))
        def _():
            start(b + 1, 0, 0)

        consume(2 * i + 1, 1)
        return carry

    lax.fori_loop(0, nch // 2, chunk_pair, 0)

    def finalize():
        nl = nl_ref[...]
        s2 = (lax.dot_general(qls_ref[...], nl, NT_DIMS, preferred_element_type=F32)
              + lax.dot_general(qrs_ref[...], nr_ref[:, :A_ROPE], NT_DIMS,
                                preferred_element_type=F32)) * C_LOG2
        rows = A_HEADS * DEC_SEQ
        tq = lax.broadcasted_iota(jnp.int32, (rows, 1), 0) & (DEC_SEQ - 1)
        tk = lax.broadcasted_iota(jnp.int32, (1, DEC_SEQ), 1)
        update(0, jnp.where(tk <= tq, s2, -jnp.inf), nl)
        m_tot = m_ref[0]
        for g in range(1, KEY_GROUPS):
            m_tot = jnp.maximum(m_tot, m_ref[g])
        l_tot = jnp.zeros((rows, 1), F32)
        acc = jnp.zeros((rows, KV_LORA), F32)
        for g in range(KEY_GROUPS):
            w = jnp.exp2(m_ref[g] - m_tot)
            l_tot = l_tot + w * l_ref[g]
            acc = acc + w * acc_ref[g]
        o = acc / l_tot
        for h in range(A_HEADS):
            o_ref[:, h * KV_LORA:(h + 1) * KV_LORA] = o[h * DEC_SEQ:(h + 1) * DEC_SEQ, :].astype(BF16)

    finalize()


def _paged_attn(page_table, ql, qr, kl, kr, cache_lat, cache_rope_t):
    ppc = PAGES_PER_STEP
    rb0 = ROW_S0 // DEC_SEQ
    rows = A_HEADS * DEC_SEQ
    assert (PAST_LEN // PAGE_SIZE) % (2 * ppc) == 0 and ppc % KEY_GROUPS == 0
    rowblk = lambda width: pl.BlockSpec((DEC_SEQ, width), lambda b, pt: (rb0 + b, 0))
    grid_spec = pltpu.PrefetchScalarGridSpec(
        num_scalar_prefetch=1,
        grid=(DEC_BATCH,),
        in_specs=[rowblk(A_HEADS * KV_LORA), rowblk(A_HEADS * ROPE_PAD), rowblk(KV_LORA), rowblk(ROPE_PAD),
                  pl.BlockSpec(memory_space=pl.ANY), pl.BlockSpec(memory_space=pl.ANY)],
        out_specs=pl.BlockSpec((DEC_SEQ, A_HEADS * KV_LORA), lambda b, pt: (b, 0)),
        scratch_shapes=[pltpu.VMEM((2, ppc, PAGE_SIZE, KV_LORA), F32),
                        pltpu.VMEM((2, ppc, A_ROPE, PAGE_SIZE), F32),
                        pltpu.SemaphoreType.DMA((2, 2)),
                        pltpu.VMEM((rows, KV_LORA), BF16), pltpu.VMEM((rows, A_ROPE), BF16),
                        pltpu.VMEM((KEY_GROUPS, rows, 1), F32), pltpu.VMEM((KEY_GROUPS, rows, 1), F32),
                        pltpu.VMEM((KEY_GROUPS, rows, KV_LORA), F32)],
    )
    return pl.pallas_call(
        _paged_attn_kernel,
        grid_spec=grid_spec,
        out_shape=jax.ShapeDtypeStruct((DEC_BATCH * DEC_SEQ, A_HEADS * KV_LORA), BF16),
        compiler_params=_cparams(("arbitrary",)),
        name="paged_attn",
    )(page_table, ql, qr, kl, kr, cache_lat, cache_rope_t)


def _uvproj_kernel(o_ref, w_ref, yin_ref, y_ref):
    del yin_ref
    for h in range(A_HEADS):
        y = jnp.dot(o_ref[:, h * KV_LORA:(h + 1) * KV_LORA], w_ref[h], preferred_element_type=F32)
        y_ref[:, h * A_VD:(h + 1) * A_VD] = y.astype(BF16)


def _uvproj_sample(o, wuv, y_in):
    rb0 = ROW_S0 // TM
    return pl.pallas_call(
        _uvproj_kernel,
        grid=(o.shape[0] // TM,),
        in_specs=[pl.BlockSpec((TM, A_HEADS * KV_LORA), lambda i: (i, 0)),
                  pl.BlockSpec(wuv.shape, lambda i: (0, 0, 0)),
                  pl.BlockSpec(memory_space=pl.ANY)],
        out_specs=pl.BlockSpec((TM, A_HEADS * A_VD), lambda i: (rb0 + i, 0)),
        out_shape=jax.ShapeDtypeStruct(y_in.shape, BF16),
        input_output_aliases={2: 0},
        compiler_params=_cparams(("parallel",)),
        name="mla_uvproj_sample",
    )(o, wuv, y_in)


def _merge_kernel(ym_ref, ya_ref, wm_ref, wa_ref, gm_ref, ga_ref, t_ref):
    y_m = jnp.dot(ym_ref[...], wm_ref[...], preferred_element_type=F32)
    y_a = jnp.dot(ya_ref[...], wa_ref[...], preferred_element_type=F32)
    t_ref[...] = (gm_ref[...] * y_m + ga_ref[...] * y_a).astype(BF16)


def _merge(ym, ya, wm, wa, og):
    tn = 1024
    nb = D_MODEL // tn
    return pl.pallas_call(
        _merge_kernel,
        grid=(ROWS // TM, nb),
        in_specs=[pl.BlockSpec((TM, D_MODEL), lambda i, j: (i, 0)),
                  pl.BlockSpec((TM, D_MODEL), lambda i, j: (i, 0)),
                  pl.BlockSpec((D_MODEL, tn), lambda i, j: (0, j)),
                  pl.BlockSpec((D_MODEL, tn), lambda i, j: (0, j)),
                  pl.BlockSpec((TM, tn), lambda i, j: (i, nb + j)),
                  pl.BlockSpec((TM, tn), lambda i, j: (i, 2 * nb + j))],
        out_specs=pl.BlockSpec((TM, tn), lambda i, j: (i, j)),
        out_shape=jax.ShapeDtypeStruct((ROWS, D_MODEL), BF16),
        compiler_params=_cparams(("parallel", "arbitrary")),
        name="branch_merge",
    )(ym, ya, wm, wa, og, og)


def _layer_norm(z, g, b):
    mu = jnp.mean(z, axis=-1, keepdims=True)
    d = z - mu
    var = jnp.mean(d * d, axis=-1, keepdims=True)
    return d * lax.rsqrt(var + LN_EPS) * g + b


def _outln_kernel(t_ref, w_ref, x_ref, g_ref, b_ref, xf_ref, xb_ref):
    mix = jnp.dot(t_ref[...], w_ref[...], preferred_element_type=F32)
    x1 = _layer_norm(ALPHA * x_ref[...] + mix, g_ref[...], b_ref[...])
    xf_ref[...] = x1
    xb_ref[...] = x1.astype(BF16)


def _outln(t, w, x, g, b):
    row = pl.BlockSpec((TM, D_MODEL), lambda i: (i, 0))
    vec = pl.BlockSpec((1, D_MODEL), lambda i: (0, 0))
    sd = jax.ShapeDtypeStruct
    return pl.pallas_call(
        _outln_kernel,
        grid=(ROWS // TM,),
        in_specs=[row, pl.BlockSpec((D_MODEL, D_MODEL), lambda i: (0, 0)), row, vec, vec],
        out_specs=[row, row],
        out_shape=[sd((ROWS, D_MODEL), F32), sd((ROWS, D_MODEL), BF16)],
        compiler_params=_cparams(("parallel",)),
        name="out_proj_ln",
    )(t, w, x, g, b)


def _mlp_kernel(xb_ref, wu_ref, wd_ref, xf_ref, g_ref, b_ref, y_ref, acc_ref):
    f = pl.program_id(1)
    h = jnp.dot(xb_ref[...], wu_ref[...], preferred_element_type=F32)
    h = jnp.square(jnp.maximum(h, 0.0)).astype(BF16)
    part = jnp.dot(h, wd_ref[...], preferred_element_type=F32)

    @pl.when(f == 0)
    def _():
        acc_ref[...] = part

    @pl.when(f > 0)
    def _():
        acc_ref[...] += part

    @pl.when(f == pl.num_programs(1) - 1)
    def _():
        y_ref[...] = _layer_norm(ALPHA * xf_ref[...] + acc_ref[...], g_ref[...], b_ref[...])


def _mlp(xb, wu, wd, xf, g, b):
    tf = 1024
    row = pl.BlockSpec((TM, D_MODEL), lambda i, f: (i, 0))
    vec = pl.BlockSpec((1, D_MODEL), lambda i, f: (0, 0))
    return pl.pallas_call(
        _mlp_kernel,
        grid=(ROWS // TM, D_FF // tf),
        in_specs=[row, pl.BlockSpec((D_MODEL, tf), lambda i, f: (0, f)),
                  pl.BlockSpec((tf, D_MODEL), lambda i, f: (f, 0)), row, vec, vec],
        out_specs=row,
        out_shape=jax.ShapeDtypeStruct((ROWS, D_MODEL), F32),
        scratch_shapes=[pltpu.VMEM((TM, D_MODEL), F32)],
        compiler_params=_cparams(("parallel", "arbitrary")),
        name="mlp_ln",
    )(xb, wu, wd, xf, g, b)


def _rot_cols(w):
    half = A_ROPE // 2
    return jnp.concatenate([-w[..., half:], w[..., :half]], axis=-1)


def _pad_cols(w, n):
    return jnp.pad(w, [(0, 0)] * (w.ndim - 1) + [(0, n - w.shape[-1])])


def _rope_tables():
    half = A_ROPE // 2
    pos = jnp.concatenate([
        N_META + jnp.arange(SEQ, dtype=jnp.int32),
        jnp.tile(PAST_LEN + jnp.arange(DEC_SEQ, dtype=jnp.int32), DEC_BATCH),
        jnp.arange(N_META, dtype=jnp.int32),
        jnp.zeros((ROWS - ROW_M0 - N_META,), jnp.int32)])
    inv = ROPE_BASE ** (-jnp.arange(half, dtype=F32) / half)
    ang = pos.astype(F32)[:, None] * inv
    cos, sin = jnp.cos(ang), jnp.sin(ang)
    return (_pad_cols(jnp.concatenate([cos, cos], axis=-1), ROPE_PAD),
            _pad_cols(jnp.concatenate([sin, sin], axis=-1), ROPE_PAD))


def kernel(x_prompt, x_sample, cache_latent, cache_krope, state_C, state_n, state_m, page_table, meta, w_in, b_in, mh_g, q_norm_g, kv_norm_g, w_uq, w_uk, w_uv, w_br_m, w_br_a, w_out, ln1_g, ln1_b, w_up, w_down, ln2_g, ln2_b):
    wi, bi = w_in[0], b_in[0]
    x_all = jnp.concatenate([
        x_prompt[0], x_sample.reshape(DEC_BATCH * DEC_SEQ, D_MODEL), meta,
        jnp.zeros((ROWS - ROW_M0 - N_META, D_MODEL), F32)], axis=0)
    cos_t, sin_t = _rope_tables()

    ones = lambda n: jnp.ones((1, n), F32)
    k_scale = jnp.concatenate([jnp.ones((1, 1024), F32), jnp.full((1, 1024), M_DK ** -0.5, F32),
                               jnp.ones((1, 2048), F32)], axis=1)
    qkv = _matmul(x_all, wi[:, OFF_Q:OFF_O].astype(BF16), bi[None, OFF_Q:OFF_O], k_scale,
                  act=None, out_dtype=BF16, tn=1024, name="inproj_qkv")
    w_og = jnp.concatenate([wi[:, OFF_O:OFF_I], wi[:, OFF_G:]], axis=1).astype(BF16)
    b_og = jnp.concatenate([bi[OFF_O:OFF_I], bi[OFF_G:]])[None]
    og = _matmul(x_all, w_og, b_og, ones(w_og.shape[1]), act="sigmoid", out_dtype=F32, tn=1024,
                 name="inproj_gates")

    def tail_cols(a):
        kr = a[..., OFF_KR:OFF_G]
        return jnp.concatenate([
            a[..., OFF_AQ:OFF_KR], _pad_cols(kr, ROPE_PAD), _pad_cols(_rot_cols(kr), ROPE_PAD),
            _pad_cols(a[..., OFF_I:OFF_AQ], LANE)], axis=-1)

    qn, ckv_f, ckv_b, ckv_t, kr_f, kr_b, gates = _tail(
        x_all, tail_cols(wi).astype(BF16), tail_cols(bi)[None], q_norm_g, kv_norm_g, cos_t, sin_t)

    mhg = mh_g
    ym = jnp.zeros((ROWS, M_HEADS * M_DV), BF16)
    zc = jnp.zeros((1, M_HEADS, M_DK, M_DV), F32)
    zn = jnp.zeros((1, M_HEADS, M_DK), F32)
    zm = jnp.zeros((1, M_HEADS, 1), F32)
    ym, c_m, n_m, m_m = _mlstm(qkv, gates, og, mhg, zc, zn, zm, ym, length=N_META, nseq=1, nsteps=1,
                               row0=ROW_M0, carry=True, name="mlstm_meta")
    ym, p_c, p_n, p_m = _mlstm(qkv, gates, og, mhg, c_m, n_m, m_m, ym, length=M_CHUNK_P, nseq=1,
                               nsteps=SEQ // M_CHUNK_P, row0=0, carry=True, name="mlstm_prompt")
    ym, s_c, s_n, s_m = _mlstm(qkv, gates, og, mhg, state_C[0], state_n[0], state_m[0][..., None], ym,
                               length=DEC_SEQ, nseq=SEQ_PER_STEP, nsteps=DEC_BATCH // SEQ_PER_STEP,
                               row0=ROW_S0, carry=False, name="mlstm_sample")

    wq = w_uq[0].reshape(Q_LORA, A_HEADS, A_NOPE + A_ROPE).transpose(1, 0, 2)
    wq_r = wq[..., A_NOPE:]
    wq_p = jnp.concatenate([wq[..., :A_NOPE], _pad_cols(wq_r, ROPE_PAD), _pad_cols(_rot_cols(wq_r), ROPE_PAD)],
                           axis=-1).astype(BF16)
    wuk = w_uk[0].transpose(1, 2, 0).astype(BF16)
    wuv = w_uv[0].transpose(1, 0, 2).astype(BF16)
    q_lat, q_rope = _qproj(qn, wq_p, wuk, cos_t, sin_t)
    ya = jnp.zeros((ROWS, A_HEADS * A_VD), BF16)
    ya = _prompt_attn(q_lat, q_rope, ckv_b, kr_b, ckv_t, ckv_t[:, ROW_M0:ROW_M0 + N_META], wuv, ya)
    o_s = _paged_attn(page_table, q_lat, q_rope, ckv_b, kr_b, cache_latent[0],
                      jnp.swapaxes(cache_krope[0], 1, 2))
    ya = _uvproj_sample(o_s, wuv, ya)

    t = _merge(ym, ya, w_br_m[0].astype(BF16), w_br_a[0].astype(BF16), og)
    x1f, x1b = _outln(t, w_out[0].astype(BF16), x_all, ln1_g, ln1_b)
    y = _mlp(x1b, w_up[0].astype(BF16), w_down[0].astype(BF16), x1f, ln2_g, ln2_b)

    def prompt_rows(a, width):
        return jnp.concatenate([a[ROW_M0:ROW_M0 + N_META, :width], a[:SEQ, :width]], axis=0)[None, None]

    def sample_rows(a, width):
        return a[ROW_S0:ROW_M0, :width].reshape(1, DEC_BATCH, DEC_SEQ, width)

    return (y[:SEQ][None], y[ROW_S0:ROW_M0].reshape(DEC_BATCH, DEC_SEQ, D_MODEL),
            prompt_rows(ckv_f, KV_LORA), prompt_rows(kr_f, A_ROPE),
            p_c[None], p_n[None], p_m[..., 0][None],
            sample_rows(ckv_f, KV_LORA), sample_rows(kr_f, A_ROPE),
            s_c[None], s_n[None], s_m[..., 0][None])
```

```python
import functools
import math

import numpy as np
import jax
import jax.numpy as jnp
from jax import lax
from jax.experimental import pallas as pl
from jax.experimental.pallas import tpu as pltpu

F32 = jnp.float32
BF16 = jnp.bfloat16

D_MODEL = 2048
SEQ = 8192
DEC_BATCH = 128
DEC_SEQ = 8
PAST_LEN = 16384
PAGE_SIZE = 128
N_META = 16
M_HEADS = 8
M_DK = 128
M_DV = 256
A_HEADS = 16
A_NOPE = 128
A_ROPE = 64
A_VD = 128
Q_LORA = 512
KV_LORA = 256
ROPE_BASE = 10000.0
ATTN_SCALE = (A_NOPE + A_ROPE) ** -0.5
D_FF = 4 * D_MODEL
ALPHA = 2.0 ** 0.25
LN_EPS = 1e-5
RMS_EPS = 1e-6

OFF_Q, OFF_K, OFF_V, OFF_O, OFF_I, OFF_F, OFF_AQ, OFF_CKV, OFF_KR, OFF_G = (
    0, 1024, 2048, 4096, 6144, 6152, 6160, 6672, 6928, 6992)
D_IN = 11088

ROW_S0 = SEQ
ROW_M0 = SEQ + DEC_BATCH * DEC_SEQ
ROWS = 9728
TM = 512
LANE = 128
ROPE_PAD = LANE

M_CHUNK_P = 256
SEQ_PER_STEP = 2
TQ = 256
TK = 512
PAGES_PER_STEP = 8
RING_SLOTS = 4
KEY_GROUPS = 2
VMEM_LIMIT = 56 * 1024 * 1024


def _cparams(sem):
    return pltpu.CompilerParams(dimension_semantics=sem, vmem_limit_bytes=VMEM_LIMIT)


def _mm_kernel(x_ref, w_ref, b_ref, s_ref, o_ref, xb_ref, *, act):
    @pl.when(pl.program_id(1) == 0)
    def _():
        xb_ref[...] = x_ref[...].astype(BF16)

    y = jnp.dot(xb_ref[...], w_ref[...], preferred_element_type=F32)
    y = (y + b_ref[...]) * s_ref[...]
    if act == "sigmoid":
        y = jax.nn.sigmoid(y)
    o_ref[...] = y.astype(o_ref.dtype)


def _matmul(x, w, b, s, *, act, out_dtype, tn, name):
    m, k = x.shape
    n = w.shape[1]
    return pl.pallas_call(
        functools.partial(_mm_kernel, act=act),
        grid=(m // TM, n // tn),
        in_specs=[
            pl.BlockSpec((TM, k), lambda i, j: (i, 0)),
            pl.BlockSpec((k, tn), lambda i, j: (0, j)),
            pl.BlockSpec((1, tn), lambda i, j: (0, j)),
            pl.BlockSpec((1, tn), lambda i, j: (0, j)),
        ],
        out_specs=pl.BlockSpec((TM, tn), lambda i, j: (i, j)),
        out_shape=jax.ShapeDtypeStruct((m, n), out_dtype),
        scratch_shapes=[pltpu.VMEM((TM, k), BF16)],
        compiler_params=_cparams(("parallel", "arbitrary")),
        name=name,
    )(x, w, b, s)


T_AQ, T_CKV, T_KR, T_KROT, T_G, T_END = 0, 512, 768, 896, 1024, 1152


def _tail_kernel(x_ref, w_ref, b_ref, qg_ref, kg_ref, cos_ref, sin_ref,
                 qn_ref, cf_ref, cb_ref, ct_ref, rf_ref, rb_ref, g_ref):
    xb = x_ref[...].astype(BF16)
    acc = jnp.dot(xb, w_ref[...], preferred_element_type=F32) + b_ref[...]
    aq = acc[:, T_AQ:T_CKV]
    qn = aq * lax.rsqrt(jnp.mean(aq * aq, axis=-1, keepdims=True) + RMS_EPS) * qg_ref[...]
    qn_ref[...] = qn.astype(BF16)
    ckv = acc[:, T_CKV:T_KR]
    c = ckv * lax.rsqrt(jnp.mean(ckv * ckv, axis=-1, keepdims=True) + RMS_EPS) * kg_ref[...]
    cf_ref[...] = c
    cb_ref[...] = c.astype(BF16)
    ct_ref[...] = c.T.astype(BF16)
    kr = acc[:, T_KR:T_KROT] * cos_ref[...] + acc[:, T_KROT:T_G] * sin_ref[...]
    rf_ref[...] = kr
    rb_ref[...] = kr.astype(BF16)
    g_ref[...] = acc[:, T_G:T_END]


def _tail(x, w, b, qg, kg, cos, sin):
    row = lambda n: pl.BlockSpec((TM, n), lambda i: (i, 0))
    full = lambda a: pl.BlockSpec(a.shape, lambda i: (0,) * a.ndim)
    sd = jax.ShapeDtypeStruct
    return pl.pallas_call(
        _tail_kernel,
        grid=(ROWS // TM,),
        in_specs=[row(D_MODEL), full(w), full(b), full(qg), full(kg), row(ROPE_PAD), row(ROPE_PAD)],
        out_specs=[row(Q_LORA), row(KV_LORA), row(KV_LORA), pl.BlockSpec((KV_LORA, TM), lambda i: (0, i)),
                   row(ROPE_PAD), row(ROPE_PAD), row(LANE)],
        out_shape=[sd((ROWS, Q_LORA), BF16), sd((ROWS, KV_LORA), F32), sd((ROWS, KV_LORA), BF16),
                   sd((KV_LORA, ROWS), BF16),
                   sd((ROWS, ROPE_PAD), F32), sd((ROWS, ROPE_PAD), BF16), sd((ROWS, LANE), F32)],
        compiler_params=_cparams(("parallel",)),
        name="inproj_tail",
    )(x, w, b, qg, kg, cos, sin)


def _log_sigmoid(x):
    return jnp.minimum(x, 0.0) - jnp.log1p(jnp.exp(-jnp.abs(x)))


def _mlstm_head(q, k, v, ig_c, ig_r, lf_c, lf_r, c0, n0, m0, length):
    ti = lax.broadcasted_iota(jnp.int32, (length, length), 0)
    si = lax.broadcasted_iota(jnp.int32, (length, length), 1)
    causal = si <= ti
    b_c = jnp.sum(jnp.where(causal, lf_r, 0.0), axis=1, keepdims=True)
    b_r = jnp.sum(jnp.where(ti <= si, lf_c, 0.0), axis=0, keepdims=True)
    dlog = jnp.where(causal, b_c - b_r + ig_r, -jnp.inf)
    m = jnp.maximum(b_c + m0, jnp.max(dlog, axis=1, keepdims=True))
    w = jnp.exp(dlog - m)
    inter = jnp.exp(b_c + m0 - m)
    s = lax.dot_general(q, k, (((1,), (1,)), ((), ())), preferred_element_type=F32) * w
    qc = jnp.dot(q, c0.astype(BF16), preferred_element_type=F32)
    num = jnp.dot(s.astype(BF16), v, preferred_element_type=F32) + inter * qc
    qn = jnp.sum(q.astype(F32) * n0, axis=1, keepdims=True)
    nq = jnp.sum(s, axis=1, keepdims=True) + inter * qn
    h = num / jnp.maximum(jnp.abs(nq), jnp.exp(-m))
    m_end = m[length - 1:length, :]
    b_end = b_c[length - 1:length, :]
    w_end = jnp.exp(b_end - b_c + ig_c - m_end)
    dec = jnp.exp(b_end + m0 - m_end)
    kw = k.astype(F32) * w_end
    c = dec * c0 + lax.dot_general(kw.astype(BF16), v, (((0,), (0,)), ((), ())),
                                   preferred_element_type=F32)
    n = dec * n0 + jnp.sum(kw, axis=0, keepdims=True)
    return h, c, n, m_end


def _mlstm_kernel(q_ref, k_ref, v_ref, g_ref, og_ref, mhg_ref, c0_ref, n0_ref, m0_ref, yin_ref,
                  y_ref, c_ref, n_ref, m_ref, cs_ref, ns_ref, ms_ref, *, length, nseq, carry):
    del yin_ref
    step = pl.program_id(0)
    if carry:
        @pl.when(step == 0)
        def _():
            cs_ref[...] = c0_ref[...]
            ns_ref[...] = n0_ref[...]
            ms_ref[...] = m0_ref[...]
        c_src, n_src, m_src = cs_ref, ns_ref, ms_ref
    else:
        c_src, n_src, m_src = c0_ref, n0_ref, m0_ref

    for b in range(nseq):
        r0 = b * length
        g = g_ref[r0:r0 + length, :]
        gt = g.T
        lfc_all = _log_sigmoid(g)
        lfr_all = _log_sigmoid(gt)
        for h in range(M_HEADS):
            q = q_ref[r0:r0 + length, h * M_DK:(h + 1) * M_DK]
            k = k_ref[r0:r0 + length, h * M_DK:(h + 1) * M_DK]
            v = v_ref[r0:r0 + length, h * M_DV:(h + 1) * M_DV]
            hh, c, n, m_end = _mlstm_head(
                q, k, v,
                g[:, h:h + 1], gt[h:h + 1, :],
                lfc_all[:, M_HEADS + h:M_HEADS + h + 1], lfr_all[M_HEADS + h:M_HEADS + h + 1, :],
                c_src[b, h], n_src[b, h:h + 1, :], m_src[b, h:h + 1, :], length)
            mu = jnp.mean(hh, axis=1, keepdims=True)
            d = hh - mu
            var = jnp.mean(d * d, axis=1, keepdims=True)
            hn = d * lax.rsqrt(var + LN_EPS)
            cols = slice(h * M_DV, (h + 1) * M_DV)
            y_ref[r0:r0 + length, cols] = (hn * mhg_ref[:, cols] * og_ref[r0:r0 + length, cols]).astype(BF16)
            if carry:
                cs_ref[b, h] = c
                ns_ref[b, h:h + 1, :] = n
                ms_ref[b, h:h + 1, :] = m_end
            else:
                c_ref[b, h] = c
                n_ref[b, h:h + 1, :] = n
                m_ref[b, h:h + 1, :] = m_end

    if carry:
        @pl.when(step == pl.num_programs(0) - 1)
        def _():
            c_ref[...] = cs_ref[...]
            n_ref[...] = ns_ref[...]
            m_ref[...] = ms_ref[...]


def _mlstm(qkv, gates, og, mhg, c0, n0, m0, y_in, *, length, nseq, nsteps, row0, carry, name):
    rows = nseq * length
    rb0 = row0 // rows
    nstate = c0.shape[0]
    if carry:
        st = lambda *tail: pl.BlockSpec((nstate,) + tail, lambda s: (0,) * (1 + len(tail)))
    else:
        st = lambda *tail: pl.BlockSpec((nseq,) + tail, lambda s: (s,) + (0,) * len(tail))
    sd = jax.ShapeDtypeStruct
    return pl.pallas_call(
        functools.partial(_mlstm_kernel, length=length, nseq=nseq, carry=carry),
        grid=(nsteps,),
        in_specs=[
            pl.BlockSpec((rows, M_HEADS * M_DK), lambda s: (rb0 + s, 0)),
            pl.BlockSpec((rows, M_HEADS * M_DK), lambda s: (rb0 + s, 1)),
            pl.BlockSpec((rows, M_HEADS * M_DV), lambda s: (rb0 + s, 1)),
            pl.BlockSpec((rows, LANE), lambda s: (rb0 + s, 0)),
            pl.BlockSpec((rows, M_HEADS * M_DV), lambda s: (rb0 + s, 0)),
            pl.BlockSpec((1, M_HEADS * M_DV), lambda s: (0, 0)),
            st(M_HEADS, M_DK, M_DV), st(M_HEADS, M_DK), st(M_HEADS, 1),
            pl.BlockSpec(memory_space=pl.ANY),
        ],
        out_specs=[
            pl.BlockSpec((rows, M_HEADS * M_DV), lambda s: (rb0 + s, 0)),
            st(M_HEADS, M_DK, M_DV), st(M_HEADS, M_DK), st(M_HEADS, 1),
        ],
        out_shape=[sd(y_in.shape, BF16), sd(c0.shape, F32), sd(n0.shape, F32), sd(m0.shape, F32)],
        scratch_shapes=[pltpu.VMEM((nstate, M_HEADS, M_DK, M_DV), F32),
                        pltpu.VMEM((nstate, M_HEADS, M_DK), F32),
                        pltpu.VMEM((nstate, M_HEADS, 1), F32)] if carry else
                       [pltpu.VMEM((8, LANE), F32)] * 3,
        input_output_aliases={9: 0},
        compiler_params=_cparams(("arbitrary",)),
        name=name,
    )(qkv, qkv, qkv, gates, og, mhg, c0, n0, m0, y_in)


QW = A_NOPE + 2 * ROPE_PAD


def _qproj_kernel(qn_ref, wq_ref, wuk_ref, cos_ref, sin_ref, ql_ref, qr_ref):
    qn = qn_ref[...]
    for h in range(A_HEADS):
        q = jnp.dot(qn, wq_ref[h], preferred_element_type=F32)
        ql = jnp.dot(q[:, :A_NOPE].astype(BF16), wuk_ref[h], preferred_element_type=F32)
        ql_ref[:, h * KV_LORA:(h + 1) * KV_LORA] = ql.astype(BF16)
        qr = q[:, A_NOPE:A_NOPE + ROPE_PAD] * cos_ref[...] + q[:, A_NOPE + ROPE_PAD:] * sin_ref[...]
        qr_ref[:, h * ROPE_PAD:(h + 1) * ROPE_PAD] = qr.astype(BF16)


def _qproj(qn, wq, wuk, cos, sin):
    sd = jax.ShapeDtypeStruct
    row = lambda n: pl.BlockSpec((TM, n), lambda i: (i, 0))
    full = lambda a: pl.BlockSpec(a.shape, lambda i: (0,) * a.ndim)
    return pl.pallas_call(
        _qproj_kernel,
        grid=(ROWS // TM,),
        in_specs=[row(Q_LORA), full(wq), full(wuk), row(ROPE_PAD), row(ROPE_PAD)],
        out_specs=[row(A_HEADS * KV_LORA), row(A_HEADS * ROPE_PAD)],
        out_shape=[sd((ROWS, A_HEADS * KV_LORA), BF16), sd((ROWS, A_HEADS * ROPE_PAD), BF16)],
        compiler_params=_cparams(("parallel",)),
        name="mla_qproj",
    )(qn, wq, wuk, cos, sin)


def _stack_heads(ref, width):
    return jnp.concatenate([ref[:, h * width:(h + 1) * width] for h in range(A_HEADS)], axis=0)


NT_DIMS = (((1,), (1,)), ((), ()))
C_LOG2 = ATTN_SCALE * math.log2(math.e)
HEADS_PER_SUB = 1
QB = HEADS_PER_SUB * TQ
NSUB = A_HEADS // HEADS_PER_SUB
SCORE_LOOKAHEAD = 2


def _pair_rows(ref, g, width):
    h0 = g * HEADS_PER_SUB
    return jnp.concatenate([ref[:, (h0 + j) * width:(h0 + j + 1) * width] for j in range(HEADS_PER_SUB)],
                           axis=0)


def _softmax_update_t(st, vt, g, m_ref, l_ref, acc_ref, init):
    mx = jnp.max(st, axis=0, keepdims=True)
    if init:
        m_new = mx
        p = jnp.exp2(st - m_new)
        l_ref[g] = jnp.sum(p, axis=0, keepdims=True)
        acc_ref[g] = jnp.dot(vt, p.astype(BF16), preferred_element_type=F32)
    else:
        m_old = m_ref[g]
        m_new = jnp.maximum(m_old, mx)
        alpha = jnp.exp2(m_old - m_new)
        p = jnp.exp2(st - m_new)
        l_ref[g] = alpha * l_ref[g] + jnp.sum(p, axis=0, keepdims=True)
        acc_ref[g] = alpha * acc_ref[g] + jnp.dot(vt, p.astype(BF16), preferred_element_type=F32)
    m_ref[g] = m_new


def _prompt_attn_kernel(qi_ref, kj_ref, first_ref, last_ref, qpos_ref,
                        ql_ref, qr_ref, kl_ref, kr_ref, vt_ref, ml_ref, mr_ref, mvt_ref, wuv_ref, yin_ref,
                        y_ref, m_ref, l_ref, acc_ref):
    del yin_ref
    s_idx = pl.program_id(0)
    qpos = qpos_ref[s_idx] + (lax.broadcasted_iota(jnp.int32, (1, QB), 1) & (TQ - 1))

    def scores(klat, krope, g):
        return (lax.dot_general(klat, _pair_rows(ql_ref, g, KV_LORA), NT_DIMS, preferred_element_type=F32)
                + lax.dot_general(krope, _pair_rows(qr_ref, g, ROPE_PAD), NT_DIMS,
                                  preferred_element_type=F32)) * C_LOG2

    def sweep(k_lat_ref, k_rope_ref, v_t_ref, kpos, init):
        ahead = [scores(k_lat_ref[...], k_rope_ref[...], g) for g in range(SCORE_LOOKAHEAD)]
        for g in range(NSUB):
            st = ahead.pop(0)
            if g + SCORE_LOOKAHEAD < NSUB:
                ahead.append(scores(k_lat_ref[...], k_rope_ref[...], g + SCORE_LOOKAHEAD))
            if kpos is not None:
                st = jnp.where(kpos <= qpos, st, -jnp.inf)
            _softmax_update_t(st, v_t_ref[...], g, m_ref, l_ref, acc_ref, init=init)

    @pl.when(first_ref[s_idx] == 1)
    def _():
        kpos = lax.broadcasted_iota(jnp.int32, (N_META, 1), 0)
        sweep(ml_ref, mr_ref, mvt_ref, kpos, init=True)

    def key_block(masked):
        kpos = N_META + kj_ref[s_idx] * TK + lax.broadcasted_iota(jnp.int32, (TK, 1), 0)
        sweep(kl_ref, kr_ref, vt_ref, kpos if masked else None, init=False)

    pl.when(last_ref[s_idx] == 0)(functools.partial(key_block, False))
    pl.when(last_ref[s_idx] == 1)(functools.partial(key_block, True))

    @pl.when(last_ref[s_idx] == 1)
    def _():
        for g in range(NSUB):
            o = (acc_ref[g] / l_ref[g]).T.astype(BF16)
            for j in range(HEADS_PER_SUB):
                h = g * HEADS_PER_SUB + j
                y = jnp.dot(o[j * TQ:(j + 1) * TQ], wuv_ref[h], preferred_element_type=F32)
                y_ref[:, h * A_VD:(h + 1) * A_VD] = y.astype(BF16)


def _prompt_schedule():
    qi, kj, first, last, qpos = [], [], [], [], []
    for i in range(SEQ // TQ):
        nk = (i * TQ + TQ - 1) // TK + 1
        for j in range(nk):
            qi.append(i); kj.append(j); first.append(int(j == 0)); last.append(int(j == nk - 1))
            qpos.append(N_META + i * TQ)
    qi.append(ROW_M0 // TQ); kj.append(0); first.append(1); last.append(1); qpos.append(0)
    return [jnp.asarray(np.asarray(a, np.int32)) for a in (qi, kj, first, last, qpos)]


def _prompt_attn(ql, qr, kl, kr, vt, mvt, wuv, y_in):
    sched = _prompt_schedule()
    nsteps = int(sched[0].shape[0])
    mblk = ROW_M0 // N_META
    const = lambda a: pl.BlockSpec(a.shape, lambda s, qi, kj, f, l, p: (0,) * a.ndim)
    grid_spec = pltpu.PrefetchScalarGridSpec(
        num_scalar_prefetch=5,
        grid=(nsteps,),
        in_specs=[
            pl.BlockSpec((TQ, A_HEADS * KV_LORA), lambda s, qi, kj, f, l, p: (qi[s], 0)),
            pl.BlockSpec((TQ, A_HEADS * ROPE_PAD), lambda s, qi, kj, f, l, p: (qi[s], 0)),
            pl.BlockSpec((TK, KV_LORA), lambda s, qi, kj, f, l, p: (kj[s], 0)),
            pl.BlockSpec((TK, ROPE_PAD), lambda s, qi, kj, f, l, p: (kj[s], 0)),
            pl.BlockSpec((KV_LORA, TK), lambda s, qi, kj, f, l, p: (0, kj[s])),
            pl.BlockSpec((N_META, KV_LORA), lambda s, qi, kj, f, l, p: (mblk, 0)),
            pl.BlockSpec((N_META, ROPE_PAD), lambda s, qi, kj, f, l, p: (mblk, 0)),
            const(mvt), const(wuv),
            pl.BlockSpec(memory_space=pl.ANY),
        ],
        out_specs=pl.BlockSpec((TQ, A_HEADS * A_VD), lambda s, qi, kj, f, l, p: (qi[s], 0)),
        scratch_shapes=[pltpu.VMEM((NSUB, 1, QB), F32), pltpu.VMEM((NSUB, 1, QB), F32),
                        pltpu.VMEM((NSUB, KV_LORA, QB), F32)],
    )
    return pl.pallas_call(
        _prompt_attn_kernel,
        grid_spec=grid_spec,
        out_shape=jax.ShapeDtypeStruct(y_in.shape, BF16),
        input_output_aliases={5 + 9: 0},
        compiler_params=_cparams(("arbitrary",)),
        name="prompt_attn",
    )(*sched, ql, qr, kl, kr, vt, kl, kr, mvt, wuv, y_in)


def _paged_attn_kernel(pt_ref, ql_ref, qr_ref, nl_ref, nr_ref, lat_hbm, rope_hbm, o_ref,
                       lat_buf, rope_buf, sem, qls_ref, qrs_ref, m_ref, l_ref, acc_ref):
    b = pl.program_id(0)
    nseq = pl.num_programs(0)
    ppc = PAGES_PER_STEP
    nch = (PAST_LEN // PAGE_SIZE) // ppc
    ahead = RING_SLOTS - 1

    def page_copies(seq, chunk, slot):
        cps = []
        for p in range(ppc):
            page = pt_ref[seq, chunk * ppc + p]
            cps.append(pltpu.make_async_copy(lat_hbm.at[page], lat_buf.at[slot, p], sem.at[0, slot]))
            cps.append(pltpu.make_async_copy(rope_hbm.at[page], rope_buf.at[slot, p], sem.at[1, slot]))
        return cps

    def start(seq, chunk, slot):
        for cp in page_copies(seq, chunk, slot):
            cp.start()

    @pl.when(b == 0)
    def _():
        for c in range(ahead):
            start(0, c, c)

    qls_ref[...] = _stack_heads(ql_ref, KV_LORA)
    qrs_ref[...] = _stack_heads(qr_ref, ROPE_PAD)[:, :A_ROPE]
    m_ref[...] = jnp.full(m_ref.shape, -jnp.inf, F32)
    l_ref[...] = jnp.zeros(l_ref.shape, F32)
    acc_ref[...] = jnp.zeros(acc_ref.shape, F32)

    def update(g, s, vals):
        m_old = m_ref[g]
        m_new = jnp.maximum(m_old, jnp.max(s, axis=1, keepdims=True))
        alpha = jnp.exp2(m_old - m_new)
        p = jnp.exp2(s - m_new)
        m_ref[g] = m_new
        l_ref[g] = alpha * l_ref[g] + jnp.sum(p, axis=1, keepdims=True)
        acc_ref[g] = alpha * acc_ref[g] + jnp.dot(p.astype(BF16), vals, preferred_element_type=F32)

    def chunk_scores(chunk):
        slot = chunk % RING_SLOTS
        for cp in page_copies(b, chunk, slot):
            cp.wait()
        kl = jnp.concatenate([lat_buf[slot, p] for p in range(ppc)], axis=0).astype(BF16)
        krt = jnp.concatenate([rope_buf[slot, p] for p in range(ppc)], axis=1).astype(BF16)
        s = (lax.dot_general(qls_ref[...], kl, NT_DIMS, preferred_element_type=F32)
             + jnp.dot(qrs_ref[...], krt, preferred_element_type=F32)) * C_LOG2
        return s, kl

    cur = chunk_scores(0)
    for c in range(nch):
        n = c + ahead
        if n < nch:
            start(b, n, n % RING_SLOTS)
        else:
            pl.when(b + 1 < nseq)(functools.partial(start, b + 1, n - nch, n % RING_SLOTS))
        nxt = chunk_scores(c + 1) if c + 1 < nch else None
        update(c % KEY_GROUPS, *cur)
        cur = nxt

    def finalize():
        nl = nl_ref[...]
        s2 = (lax.dot_general(qls_ref[...], nl, NT_DIMS, preferred_element_type=F32)
              + lax.dot_general(qrs_ref[...], nr_ref[:, :A_ROPE], NT_DIMS,
                                preferred_element_type=F32)) * C_LOG2
        rows = A_HEADS * DEC_SEQ
        tq = lax.broadcasted_iota(jnp.int32, (rows, 1), 0) & (DEC_SEQ - 1)
        tk = lax.broadcasted_iota(jnp.int32, (1, DEC_SEQ), 1)
        update(0, jnp.where(tk <= tq, s2, -jnp.inf), nl)
        m_tot = m_ref[0]
        for g in range(1, KEY_GROUPS):
            m_tot = jnp.maximum(m_tot, m_ref[g])
        l_tot = jnp.zeros((rows, 1), F32)
        acc = jnp.zeros((rows, KV_LORA), F32)
        for g in range(KEY_GROUPS):
            w = jnp.exp2(m_ref[g] - m_tot)
            l_tot = l_tot + w * l_ref[g]
            acc = acc + w * acc_ref[g]
        o = acc / l_tot
        for h in range(A_HEADS):
            o_ref[:, h * KV_LORA:(h + 1) * KV_LORA] = o[h * DEC_SEQ:(h + 1) * DEC_SEQ, :].astype(BF16)

    finalize()


def _paged_attn(page_table, ql, qr, kl, kr, cache_lat, cache_rope_t):
    ppc = PAGES_PER_STEP
    rb0 = ROW_S0 // DEC_SEQ
    rows = A_HEADS * DEC_SEQ
    assert (PAST_LEN // PAGE_SIZE) % (RING_SLOTS * ppc) == 0
    rowblk = lambda width: pl.BlockSpec((DEC_SEQ, width), lambda b, pt: (rb0 + b, 0))
    grid_spec = pltpu.PrefetchScalarGridSpec(
        num_scalar_prefetch=1,
        grid=(DEC_BATCH,),
        in_specs=[rowblk(A_HEADS * KV_LORA), rowblk(A_HEADS * ROPE_PAD), rowblk(KV_LORA), rowblk(ROPE_PAD),
                  pl.BlockSpec(memory_space=pl.ANY), pl.BlockSpec(memory_space=pl.ANY)],
        out_specs=pl.BlockSpec((DEC_SEQ, A_HEADS * KV_LORA), lambda b, pt: (b, 0)),
        scratch_shapes=[pltpu.VMEM((RING_SLOTS, ppc, PAGE_SIZE, KV_LORA), F32),
                        pltpu.VMEM((RING_SLOTS, ppc, A_ROPE, PAGE_SIZE), F32),
                        pltpu.SemaphoreType.DMA((2, RING_SLOTS)),
                        pltpu.VMEM((rows, KV_LORA), BF16), pltpu.VMEM((rows, A_ROPE), BF16),
                        pltpu.VMEM((KEY_GROUPS, rows, 1), F32), pltpu.VMEM((KEY_GROUPS, rows, 1), F32),
                        pltpu.VMEM((KEY_GROUPS, rows, KV_LORA), F32)],
    )
    return pl.pallas_call(
        _paged_attn_kernel,
        grid_spec=grid_spec,
        out_shape=jax.ShapeDtypeStruct((DEC_BATCH * DEC_SEQ, A_HEADS * KV_LORA), BF16),
        compiler_params=_cparams(("arbitrary",)),
        name="paged_attn",
    )(page_table, ql, qr, kl, kr, cache_lat, cache_rope_t)


def _uvproj_kernel(o_ref, w_ref, yin_ref, y_ref):
    del yin_ref
    for h in range(A_HEADS):
        y = jnp.dot(o_ref[:, h * KV_LORA:(h + 1) * KV_LORA], w_ref[h], preferred_element_type=F32)
        y_ref[:, h * A_VD:(h + 1) * A_VD] = y.astype(BF16)


def _uvproj_sample(o, wuv, y_in):
    rb0 = ROW_S0 // TM
    return pl.pallas_call(
        _uvproj_kernel,
        grid=(o.shape[0] // TM,),
        in_specs=[pl.BlockSpec((TM, A_HEADS * KV_LORA), lambda i: (i, 0)),
                  pl.BlockSpec(wuv.shape, lambda i: (0, 0, 0)),
                  pl.BlockSpec(memory_space=pl.ANY)],
        out_specs=pl.BlockSpec((TM, A_HEADS * A_VD), lambda i: (rb0 + i, 0)),
        out_shape=jax.ShapeDtypeStruct(y_in.shape, BF16),
        input_output_aliases={2: 0},
        compiler_params=_cparams(("parallel",)),
        name="mla_uvproj_sample",
    )(o, wuv, y_in)


def _merge_kernel(ym_ref, ya_ref, wm_ref, wa_ref, gm_ref, ga_ref, t_ref):
    y_m = jnp.dot(ym_ref[...], wm_ref[...], preferred_element_type=F32)
    y_a = jnp.dot(ya_ref[...], wa_ref[...], preferred_element_type=F32)
    t_ref[...] = (gm_ref[...] * y_m + ga_ref[...] * y_a).astype(BF16)


def _merge(ym, ya, wm, wa, og):
    tn = 1024
    nb = D_MODEL // tn
    return pl.pallas_call(
        _merge_kernel,
        grid=(ROWS // TM, nb),
        in_specs=[pl.BlockSpec((TM, D_MODEL), lambda i, j: (i, 0)),
                  pl.BlockSpec((TM, D_MODEL), lambda i, j: (i, 0)),
                  pl.BlockSpec((D_MODEL, tn), lambda i, j: (0, j)),
                  pl.BlockSpec((D_MODEL, tn), lambda i, j: (0, j)),
                  pl.BlockSpec((TM, tn), lambda i, j: (i, nb + j)),
                  pl.BlockSpec((TM, tn), lambda i, j: (i, 2 * nb + j))],
        out_specs=pl.BlockSpec((TM, tn), lambda i, j: (i, j)),
        out_shape=jax.ShapeDtypeStruct((ROWS, D_MODEL), BF16),
        compiler_params=_cparams(("parallel", "arbitrary")),
        name="branch_merge",
    )(ym, ya, wm, wa, og, og)


def _layer_norm(z, g, b):
    mu = jnp.mean(z, axis=-1, keepdims=True)
    d = z - mu
    var = jnp.mean(d * d, axis=-1, keepdims=True)
    return d * lax.rsqrt(var + LN_EPS) * g + b


def _outln_kernel(t_ref, w_ref, x_ref, g_ref, b_ref, xf_ref, xb_ref):
    mix = jnp.dot(t_ref[...], w_ref[...], preferred_element_type=F32)
    x1 = _layer_norm(ALPHA * x_ref[...] + mix, g_ref[...], b_ref[...])
    xf_ref[...] = x1
    xb_ref[...] = x1.astype(BF16)


def _outln(t, w, x, g, b):
    row = pl.BlockSpec((TM, D_MODEL), lambda i: (i, 0))
    vec = pl.BlockSpec((1, D_MODEL), lambda i: (0, 0))
    sd = jax.ShapeDtypeStruct
    return pl.pallas_call(
        _outln_kernel,
        grid=(ROWS // TM,),
        in_specs=[row, pl.BlockSpec((D_MODEL, D_MODEL), lambda i: (0, 0)), row, vec, vec],
        out_specs=[row, row],
        out_shape=[sd((ROWS, D_MODEL), F32), sd((ROWS, D_MODEL), BF16)],
        compiler_params=_cparams(("parallel",)),
        name="out_proj_ln",
    )(t, w, x, g, b)


def _mlp_kernel(xb_ref, wu_ref, wd_ref, xf_ref, g_ref, b_ref, y_ref, acc_ref):
    f = pl.program_id(1)
    h = jnp.dot(xb_ref[...], wu_ref[...], preferred_element_type=F32)
    h = jnp.square(jnp.maximum(h, 0.0)).astype(BF16)
    part = jnp.dot(h, wd_ref[...], preferred_element_type=F32)

    @pl.when(f == 0)
    def _():
        acc_ref[...] = part

    @pl.when(f > 0)
    def _():
        acc_ref[...] += part

    @pl.when(f == pl.num_programs(1) - 1)
    def _():
        y_ref[...] = _layer_norm(ALPHA * xf_ref[...] + acc_ref[...], g_ref[...], b_ref[...])


def _mlp(xb, wu, wd, xf, g, b):
    tf = 1024
    row = pl.BlockSpec((TM, D_MODEL), lambda i, f: (i, 0))
    vec = pl.BlockSpec((1, D_MODEL), lambda i, f: (0, 0))
    return pl.pallas_call(
        _mlp_kernel,
        grid=(ROWS // TM, D_FF // tf),
        in_specs=[row, pl.BlockSpec((D_MODEL, tf), lambda i, f: (0, f)),
                  pl.BlockSpec((tf, D_MODEL), lambda i, f: (f, 0)), row, vec, vec],
        out_specs=row,
        out_shape=jax.ShapeDtypeStruct((ROWS, D_MODEL), F32),
        scratch_shapes=[pltpu.VMEM((TM, D_MODEL), F32)],
        compiler_params=_cparams(("parallel", "arbitrary")),
        name="mlp_ln",
    )(xb, wu, wd, xf, g, b)


def _rot_cols(w):
    half = A_ROPE // 2
    return jnp.concatenate([-w[..., half:], w[..., :half]], axis=-1)


def _pad_cols(w, n):
    return jnp.pad(w, [(0, 0)] * (w.ndim - 1) + [(0, n - w.shape[-1])])


def _rope_tables():
    half = A_ROPE // 2
    pos = jnp.concatenate([
        N_META + jnp.arange(SEQ, dtype=jnp.int32),
        jnp.tile(PAST_LEN + jnp.arange(DEC_SEQ, dtype=jnp.int32), DEC_BATCH),
        jnp.arange(N_META, dtype=jnp.int32),
        jnp.zeros((ROWS - ROW_M0 - N_META,), jnp.int32)])
    inv = ROPE_BASE ** (-jnp.arange(half, dtype=F32) / half)
    ang = pos.astype(F32)[:, None] * inv
    cos, sin = jnp.cos(ang), jnp.sin(ang)
    return (_pad_cols(jnp.concatenate([cos, cos], axis=-1), ROPE_PAD),
            _pad_cols(jnp.concatenate([sin, sin], axis=-1), ROPE_PAD))


def kernel(x_prompt, x_sample, cache_latent, cache_krope, state_C, state_n, state_m, page_table, meta, w_in, b_in, mh_g, q_norm_g, kv_norm_g, w_uq, w_uk, w_uv, w_br_m, w_br_a, w_out, ln1_g, ln1_b, w_up, w_down, ln2_g, ln2_b):
    wi, bi = w_in[0], b_in[0]
    x_all = jnp.concatenate([
        x_prompt[0], x_sample.reshape(DEC_BATCH * DEC_SEQ, D_MODEL), meta,
        jnp.zeros((ROWS - ROW_M0 - N_META, D_MODEL), F32)], axis=0)
    cos_t, sin_t = _rope_tables()

    ones = lambda n: jnp.ones((1, n), F32)
    k_scale = jnp.concatenate([jnp.ones((1, 1024), F32), jnp.full((1, 1024), M_DK ** -0.5, F32),
                               jnp.ones((1, 2048), F32)], axis=1)
    qkv = _matmul(x_all, wi[:, OFF_Q:OFF_O].astype(BF16), bi[None, OFF_Q:OFF_O], k_scale,
                  act=None, out_dtype=BF16, tn=2048, name="inproj_qkv")
    w_og = jnp.concatenate([wi[:, OFF_O:OFF_I], wi[:, OFF_G:]], axis=1).astype(BF16)
    b_og = jnp.concatenate([bi[OFF_O:OFF_I], bi[OFF_G:]])[None]
    og = _matmul(x_all, w_og, b_og, ones(w_og.shape[1]), act="sigmoid", out_dtype=F32, tn=2048,
                 name="inproj_gates")

    def tail_cols(a):
        kr = a[..., OFF_KR:OFF_G]
        return jnp.concatenate([
            a[..., OFF_AQ:OFF_KR], _pad_cols(kr, ROPE_PAD), _pad_cols(_rot_cols(kr), ROPE_PAD),
            _pad_cols(a[..., OFF_I:OFF_AQ], LANE)], axis=-1)

    qn, ckv_f, ckv_b, ckv_t, kr_f, kr_b, gates = _tail(
        x_all, tail_cols(wi).astype(BF16), tail_cols(bi)[None], q_norm_g, kv_norm_g, cos_t, sin_t)

    mhg = mh_g
    ym = jnp.zeros((ROWS, M_HEADS * M_DV), BF16)
    zc = jnp.zeros((1, M_HEADS, M_DK, M_DV), F32)
    zn = jnp.zeros((1, M_HEADS, M_DK), F32)
    zm = jnp.zeros((1, M_HEADS, 1), F32)
    ym, c_m, n_m, m_m = _mlstm(qkv, gates, og, mhg, zc, zn, zm, ym, length=N_META, nseq=1, nsteps=1,
                               row0=ROW_M0, carry=True, name="mlstm_meta")
    ym, p_c, p_n, p_m = _mlstm(qkv, gates, og, mhg, c_m, n_m, m_m, ym, length=M_CHUNK_P, nseq=1,
                               nsteps=SEQ // M_CHUNK_P, row0=0, carry=True, name="mlstm_prompt")
    ym, s_c, s_n, s_m = _mlstm(qkv, gates, og, mhg, state_C[0], state_n[0], state_m[0][..., None], ym,
                               length=DEC_SEQ, nseq=SEQ_PER_STEP, nsteps=DEC_BATCH // SEQ_PER_STEP,
                               row0=ROW_S0, carry=False, name="mlstm_sample")

    wq = w_uq[0].reshape(Q_LORA, A_HEADS, A_NOPE + A_ROPE).transpose(1, 0, 2)
    wq_r = wq[..., A_NOPE:]
    wq_p = jnp.concatenate([wq[..., :A_NOPE], _pad_cols(wq_r, ROPE_PAD), _pad_cols(_rot_cols(wq_r), ROPE_PAD)],
                           axis=-1).astype(BF16)
    wuk = w_uk[0].transpose(1, 2, 0).astype(BF16)
    wuv = w_uv[0].transpose(1, 0, 2).astype(BF16)
    q_lat, q_rope = _qproj(qn, wq_p, wuk, cos_t, sin_t)
    ya = jnp.zeros((ROWS, A_HEADS * A_VD), BF16)
    ya = _prompt_attn(q_lat, q_rope, ckv_b, kr_b, ckv_t, ckv_t[:, ROW_M0:ROW_M0 + N_META], wuv, ya)
    o_s = _paged_attn(page_table, q_lat, q_rope, ckv_b, kr_b, cache_latent[0],
                      jnp.swapaxes(cache_krope[0], 1, 2))
    ya = _uvproj_sample(o_s, wuv, ya)

    t = _merge(ym, ya, w_br_m[0].astype(BF16), w_br_a[0].astype(BF16), og)
    x1f, x1b = _outln(t, w_out[0].astype(BF16), x_all, ln1_g, ln1_b)
    y = _mlp(x1b, w_up[0].astype(BF16), w_down[0].astype(BF16), x1f, ln2_g, ln2_b)

    def prompt_rows(a, width):
        return jnp.concatenate([a[ROW_M0:ROW_M0 + N_META, :width], a[:SEQ, :width]], axis=0)[None, None]

    def sample_rows(a, width):
        return a[ROW_S0:ROW_M0, :width].reshape(1, DEC_BATCH, DEC_SEQ, width)

    return (y[:SEQ][None], y[ROW_S0:ROW_M0].reshape(DEC_BATCH, DEC_SEQ, D_MODEL),
            prompt_rows(ckv_f, KV_LORA), prompt_rows(kr_f, A_ROPE),
            p_c[None], p_n[None], p_m[..., 0][None],
            sample_rows(ckv_f, KV_LORA), sample_rows(kr_f, A_ROPE),
            s_c[None], s_n[None], s_m[..., 0][None])
```

```python
import functools
import math

import numpy as np
import jax
import jax.numpy as jnp
from jax import lax
from jax.experimental import pallas as pl
from jax.experimental.pallas import tpu as pltpu

F32 = jnp.float32
BF16 = jnp.bfloat16

D_MODEL = 2048
SEQ = 8192
DEC_BATCH = 128
DEC_SEQ = 8
PAST_LEN = 16384
PAGE_SIZE = 128
N_META = 16
M_HEADS = 8
M_DK = 128
M_DV = 256
A_HEADS = 16
A_NOPE = 128
A_ROPE = 64
A_VD = 128
Q_LORA = 512
KV_LORA = 256
ROPE_BASE = 10000.0
ATTN_SCALE = (A_NOPE + A_ROPE) ** -0.5
D_FF = 4 * D_MODEL
ALPHA = 2.0 ** 0.25
LN_EPS = 1e-5
RMS_EPS = 1e-6

OFF_Q, OFF_K, OFF_V, OFF_O, OFF_I, OFF_F, OFF_AQ, OFF_CKV, OFF_KR, OFF_G = (
    0, 1024, 2048, 4096, 6144, 6152, 6160, 6672, 6928, 6992)
D_IN = 11088

ROW_S0 = SEQ
ROW_M0 = SEQ + DEC_BATCH * DEC_SEQ
ROWS = 9728
TM = 512
LANE = 128
ROPE_PAD = LANE

M_CHUNK_P = 256
SEQ_PER_STEP = 2
TQ = 256
TK = 512
PAGES_PER_STEP = 8
RING_SLOTS = 4
KEY_GROUPS = 2
VMEM_LIMIT = 56 * 1024 * 1024


def _cparams(sem):
    return pltpu.CompilerParams(dimension_semantics=sem, vmem_limit_bytes=VMEM_LIMIT)


def _mm_kernel(x_ref, w_ref, b_ref, s_ref, o_ref, xb_ref, *, act):
    @pl.when(pl.program_id(1) == 0)
    def _():
        xb_ref[...] = x_ref[...].astype(BF16)

    y = jnp.dot(xb_ref[...], w_ref[...], preferred_element_type=F32)
    y = (y + b_ref[...]) * s_ref[...]
    if act == "sigmoid":
        y = jax.nn.sigmoid(y)
    o_ref[...] = y.astype(o_ref.dtype)


def _matmul(x, w, b, s, *, act, out_dtype, tn, name):
    m, k = x.shape
    n = w.shape[1]
    return pl.pallas_call(
        functools.partial(_mm_kernel, act=act),
        grid=(m // TM, n // tn),
        in_specs=[
            pl.BlockSpec((TM, k), lambda i, j: (i, 0)),
            pl.BlockSpec((k, tn), lambda i, j: (0, j)),
            pl.BlockSpec((1, tn), lambda i, j: (0, j)),
            pl.BlockSpec((1, tn), lambda i, j: (0, j)),
        ],
        out_specs=pl.BlockSpec((TM, tn), lambda i, j: (i, j)),
        out_shape=jax.ShapeDtypeStruct((m, n), out_dtype),
        scratch_shapes=[pltpu.VMEM((TM, k), BF16)],
        compiler_params=_cparams(("parallel", "arbitrary")),
        name=name,
    )(x, w, b, s)


T_AQ, T_CKV, T_KR, T_KROT, T_G, T_END = 0, 512, 768, 896, 1024, 1152


def _tail_kernel(x_ref, w_ref, b_ref, qg_ref, kg_ref, cos_ref, sin_ref,
                 qn_ref, cf_ref, cb_ref, rf_ref, rb_ref, g_ref):
    xb = x_ref[...].astype(BF16)
    acc = jnp.dot(xb, w_ref[...], preferred_element_type=F32) + b_ref[...]
    aq = acc[:, T_AQ:T_CKV]
    qn = aq * lax.rsqrt(jnp.mean(aq * aq, axis=-1, keepdims=True) + RMS_EPS) * qg_ref[...]
    qn_ref[...] = qn.astype(BF16)
    ckv = acc[:, T_CKV:T_KR]
    c = ckv * lax.rsqrt(jnp.mean(ckv * ckv, axis=-1, keepdims=True) + RMS_EPS) * kg_ref[...]
    cf_ref[...] = c
    cb_ref[...] = c.astype(BF16)
    kr = acc[:, T_KR:T_KROT] * cos_ref[...] + acc[:, T_KROT:T_G] * sin_ref[...]
    rf_ref[...] = kr
    rb_ref[...] = kr.astype(BF16)
    g_ref[...] = acc[:, T_G:T_END]


def _tail(x, w, b, qg, kg, cos, sin):
    row = lambda n: pl.BlockSpec((TM, n), lambda i: (i, 0))
    full = lambda a: pl.BlockSpec(a.shape, lambda i: (0,) * a.ndim)
    sd = jax.ShapeDtypeStruct
    return pl.pallas_call(
        _tail_kernel,
        grid=(ROWS // TM,),
        in_specs=[row(D_MODEL), full(w), full(b), full(qg), full(kg), row(ROPE_PAD), row(ROPE_PAD)],
        out_specs=[row(Q_LORA), row(KV_LORA), row(KV_LORA), row(ROPE_PAD), row(ROPE_PAD), row(LANE)],
        out_shape=[sd((ROWS, Q_LORA), BF16), sd((ROWS, KV_LORA), F32), sd((ROWS, KV_LORA), BF16),
                   sd((ROWS, ROPE_PAD), F32), sd((ROWS, ROPE_PAD), BF16), sd((ROWS, LANE), F32)],
        compiler_params=_cparams(("parallel",)),
        name="inproj_tail",
    )(x, w, b, qg, kg, cos, sin)


def _log_sigmoid(x):
    return jnp.minimum(x, 0.0) - jnp.log1p(jnp.exp(-jnp.abs(x)))


def _mlstm_head(q, k, v, ig_c, ig_r, lf_c, lf_r, c0, n0, m0, length):
    ti = lax.broadcasted_iota(jnp.int32, (length, length), 0)
    si = lax.broadcasted_iota(jnp.int32, (length, length), 1)
    causal = si <= ti
    b_c = jnp.sum(jnp.where(causal, lf_r, 0.0), axis=1, keepdims=True)
    b_r = jnp.sum(jnp.where(ti <= si, lf_c, 0.0), axis=0, keepdims=True)
    dlog = jnp.where(causal, b_c - b_r + ig_r, -jnp.inf)
    m = jnp.maximum(b_c + m0, jnp.max(dlog, axis=1, keepdims=True))
    w = jnp.exp(dlog - m)
    inter = jnp.exp(b_c + m0 - m)
    s = lax.dot_general(q, k, (((1,), (1,)), ((), ())), preferred_element_type=F32) * w
    qc = jnp.dot(q, c0.astype(BF16), preferred_element_type=F32)
    num = jnp.dot(s.astype(BF16), v, preferred_element_type=F32) + inter * qc
    qn = jnp.sum(q.astype(F32) * n0, axis=1, keepdims=True)
    nq = jnp.sum(s, axis=1, keepdims=True) + inter * qn
    h = num / jnp.maximum(jnp.abs(nq), jnp.exp(-m))
    m_end = m[length - 1:length, :]
    b_end = b_c[length - 1:length, :]
    w_end = jnp.exp(b_end - b_c + ig_c - m_end)
    dec = jnp.exp(b_end + m0 - m_end)
    kw = k.astype(F32) * w_end
    c = dec * c0 + lax.dot_general(kw.astype(BF16), v, (((0,), (0,)), ((), ())),
                                   preferred_element_type=F32)
    n = dec * n0 + jnp.sum(kw, axis=0, keepdims=True)
    return h, c, n, m_end


def _mlstm_kernel(q_ref, k_ref, v_ref, g_ref, og_ref, mhg_ref, c0_ref, n0_ref, m0_ref, yin_ref,
                  y_ref, c_ref, n_ref, m_ref, cs_ref, ns_ref, ms_ref, *, length, nseq, carry):
    del yin_ref
    step = pl.program_id(0)
    if carry:
        @pl.when(step == 0)
        def _():
            cs_ref[...] = c0_ref[...]
            ns_ref[...] = n0_ref[...]
            ms_ref[...] = m0_ref[...]
        c_src, n_src, m_src = cs_ref, ns_ref, ms_ref
    else:
        c_src, n_src, m_src = c0_ref, n0_ref, m0_ref

    for b in range(nseq):
        r0 = b * length
        g = g_ref[r0:r0 + length, :]
        gt = g.T
        lfc_all = _log_sigmoid(g)
        lfr_all = _log_sigmoid(gt)
        for h in range(M_HEADS):
            q = q_ref[r0:r0 + length, h * M_DK:(h + 1) * M_DK]
            k = k_ref[r0:r0 + length, h * M_DK:(h + 1) * M_DK]
            v = v_ref[r0:r0 + length, h * M_DV:(h + 1) * M_DV]
            hh, c, n, m_end = _mlstm_head(
                q, k, v,
                g[:, h:h + 1], gt[h:h + 1, :],
                lfc_all[:, M_HEADS + h:M_HEADS + h + 1], lfr_all[M_HEADS + h:M_HEADS + h + 1, :],
                c_src[b, h], n_src[b, h:h + 1, :], m_src[b, h:h + 1, :], length)
            mu = jnp.mean(hh, axis=1, keepdims=True)
            d = hh - mu
            var = jnp.mean(d * d, axis=1, keepdims=True)
            hn = d * lax.rsqrt(var + LN_EPS)
            cols = slice(h * M_DV, (h + 1) * M_DV)
            y_ref[r0:r0 + length, cols] = (hn * mhg_ref[:, cols] * og_ref[r0:r0 + length, cols]).astype(BF16)
            if carry:
                cs_ref[b, h] = c
                ns_ref[b, h:h + 1, :] = n
                ms_ref[b, h:h + 1, :] = m_end
            else:
                c_ref[b, h] = c
                n_ref[b, h:h + 1, :] = n
                m_ref[b, h:h + 1, :] = m_end

    if carry:
        @pl.when(step == pl.num_programs(0) - 1)
        def _():
            c_ref[...] = cs_ref[...]
            n_ref[...] = ns_ref[...]
            m_ref[...] = ms_ref[...]


def _mlstm(qkv, gates, og, mhg, c0, n0, m0, y_in, *, length, nseq, nsteps, row0, carry, name):
    rows = nseq * length
    rb0 = row0 // rows
    nstate = c0.shape[0]
    if carry:
        st = lambda *tail: pl.BlockSpec((nstate,) + tail, lambda s: (0,) * (1 + len(tail)))
    else:
        st = lambda *tail: pl.BlockSpec((nseq,) + tail, lambda s: (s,) + (0,) * len(tail))
    sd = jax.ShapeDtypeStruct
    return pl.pallas_call(
        functools.partial(_mlstm_kernel, length=length, nseq=nseq, carry=carry),
        grid=(nsteps,),
        in_specs=[
            pl.BlockSpec((rows, M_HEADS * M_DK), lambda s: (rb0 + s, 0)),
            pl.BlockSpec((rows, M_HEADS * M_DK), lambda s: (rb0 + s, 1)),
            pl.BlockSpec((rows, M_HEADS * M_DV), lambda s: (rb0 + s, 1)),
            pl.BlockSpec((rows, LANE), lambda s: (rb0 + s, 0)),
            pl.BlockSpec((rows, M_HEADS * M_DV), lambda s: (rb0 + s, 0)),
            pl.BlockSpec((1, M_HEADS * M_DV), lambda s: (0, 0)),
            st(M_HEADS, M_DK, M_DV), st(M_HEADS, M_DK), st(M_HEADS, 1),
            pl.BlockSpec(memory_space=pl.ANY),
        ],
        out_specs=[
            pl.BlockSpec((rows, M_HEADS * M_DV), lambda s: (rb0 + s, 0)),
            st(M_HEADS, M_DK, M_DV), st(M_HEADS, M_DK), st(M_HEADS, 1),
        ],
        out_shape=[sd(y_in.shape, BF16), sd(c0.shape, F32), sd(n0.shape, F32), sd(m0.shape, F32)],
        scratch_shapes=[pltpu.VMEM((nstate, M_HEADS, M_DK, M_DV), F32),
                        pltpu.VMEM((nstate, M_HEADS, M_DK), F32),
                        pltpu.VMEM((nstate, M_HEADS, 1), F32)] if carry else
                       [pltpu.VMEM((8, LANE), F32)] * 3,
        input_output_aliases={9: 0},
        compiler_params=_cparams(("arbitrary",)),
        name=name,
    )(qkv, qkv, qkv, gates, og, mhg, c0, n0, m0, y_in)


QW = A_NOPE + 2 * ROPE_PAD


HEAD_W = A_NOPE + ROPE_PAD


def _qproj_kernel(qn_ref, wq_ref, wuk_ref, cos_ref, sin_ref, ql_ref, qr_ref, qh_ref):
    qn = qn_ref[...]
    for h in range(A_HEADS):
        q = jnp.dot(qn, wq_ref[h], preferred_element_type=F32)
        q_nope = q[:, :A_NOPE].astype(BF16)
        ql = jnp.dot(q_nope, wuk_ref[h], preferred_element_type=F32)
        ql_ref[:, h * KV_LORA:(h + 1) * KV_LORA] = ql.astype(BF16)
        qr_f = q[:, A_NOPE:A_NOPE + ROPE_PAD] * cos_ref[...] + q[:, A_NOPE + ROPE_PAD:] * sin_ref[...]
        qr_ref[:, h * ROPE_PAD:(h + 1) * ROPE_PAD] = qr_f.astype(BF16)
        qh_ref[:, h * HEAD_W:h * HEAD_W + A_NOPE] = (q[:, :A_NOPE] * C_LOG2).astype(BF16)
        qh_ref[:, h * HEAD_W + A_NOPE:(h + 1) * HEAD_W] = (qr_f * C_LOG2).astype(BF16)


def _qproj(qn, wq, wuk, cos, sin):
    sd = jax.ShapeDtypeStruct
    row = lambda n: pl.BlockSpec((TM, n), lambda i: (i, 0))
    full = lambda a: pl.BlockSpec(a.shape, lambda i: (0,) * a.ndim)
    return pl.pallas_call(
        _qproj_kernel,
        grid=(ROWS // TM,),
        in_specs=[row(Q_LORA), full(wq), full(wuk), row(ROPE_PAD), row(ROPE_PAD)],
        out_specs=[row(A_HEADS * KV_LORA), row(A_HEADS * ROPE_PAD), row(A_HEADS * HEAD_W)],
        out_shape=[sd((ROWS, A_HEADS * KV_LORA), BF16), sd((ROWS, A_HEADS * ROPE_PAD), BF16),
                   sd((ROWS, A_HEADS * HEAD_W), BF16)],
        compiler_params=_cparams(("parallel",)),
        name="mla_qproj",
    )(qn, wq, wuk, cos, sin)


def _kvproj_kernel(c_ref, kr_ref, wk_ref, wvt_ref, k_ref, vt_ref):
    c = c_ref[...]
    kr = kr_ref[...]
    for h in range(A_HEADS):
        kn = jnp.dot(c, wk_ref[h], preferred_element_type=F32)
        k_ref[:, h * HEAD_W:h * HEAD_W + A_NOPE] = kn.astype(BF16)
        k_ref[:, h * HEAD_W + A_NOPE:(h + 1) * HEAD_W] = kr
        vt = lax.dot_general(wvt_ref[h], c, NT_DIMS, preferred_element_type=F32)
        vt_ref[h * A_VD:(h + 1) * A_VD, :] = vt.astype(BF16)


def _kvproj(c, kr, wk, wvt):
    sd = jax.ShapeDtypeStruct
    row = lambda n: pl.BlockSpec((TM, n), lambda i: (i, 0))
    full = lambda a: pl.BlockSpec(a.shape, lambda i: (0,) * a.ndim)
    return pl.pallas_call(
        _kvproj_kernel,
        grid=(ROWS // TM,),
        in_specs=[row(KV_LORA), row(ROPE_PAD), full(wk), full(wvt)],
        out_specs=[row(A_HEADS * HEAD_W), pl.BlockSpec((A_HEADS * A_VD, TM), lambda i: (0, i))],
        out_shape=[sd((ROWS, A_HEADS * HEAD_W), BF16), sd((A_HEADS * A_VD, ROWS), BF16)],
        compiler_params=_cparams(("parallel",)),
        name="mla_kvproj",
    )(c, kr, wk, wvt)


def _stack_heads(ref, width):
    return jnp.concatenate([ref[:, h * width:(h + 1) * width] for h in range(A_HEADS)], axis=0)


NT_DIMS = (((1,), (1,)), ((), ()))
C_LOG2 = ATTN_SCALE * math.log2(math.e)
SCORE_LOOKAHEAD = 3


def _softmax_update_t(st, vt, g, m_ref, l_ref, acc_ref, init):
    mx = jnp.max(st, axis=0, keepdims=True)
    if init:
        m_new = mx
        p = jnp.exp2(st - m_new)
        l_ref[g] = jnp.sum(p, axis=0, keepdims=True)
        acc_ref[g] = jnp.dot(vt, p.astype(BF16), preferred_element_type=F32)
    else:
        m_old = m_ref[g]
        m_new = jnp.maximum(m_old, mx)
        alpha = jnp.exp2(m_old - m_new)
        p = jnp.exp2(st - m_new)
        l_ref[g] = alpha * l_ref[g] + jnp.sum(p, axis=0, keepdims=True)
        acc_ref[g] = alpha * acc_ref[g] + jnp.dot(vt, p.astype(BF16), preferred_element_type=F32)
    m_ref[g] = m_new


def _prompt_attn_kernel(qi_ref, kj_ref, first_ref, last_ref, qpos_ref,
                        q_ref, k_ref, vt_ref, mk_ref, mvt_ref, yin_ref,
                        y_ref, m_ref, l_ref, acc_ref):
    del yin_ref
    s_idx = pl.program_id(0)
    qpos = qpos_ref[s_idx] + lax.broadcasted_iota(jnp.int32, (1, TQ), 1)

    def scores(keys_ref, h):
        cols = slice(h * HEAD_W, (h + 1) * HEAD_W)
        return lax.dot_general(keys_ref[:, cols], q_ref[:, cols], NT_DIMS,
                               preferred_element_type=F32)

    def sweep(keys_ref, v_t_ref, kpos, init):
        ahead = [scores(keys_ref, h) for h in range(SCORE_LOOKAHEAD)]
        for h in range(A_HEADS):
            st = ahead.pop(0)
            if h + SCORE_LOOKAHEAD < A_HEADS:
                ahead.append(scores(keys_ref, h + SCORE_LOOKAHEAD))
            if kpos is not None:
                st = jnp.where(kpos <= qpos, st, -jnp.inf)
            _softmax_update_t(st, v_t_ref[h * A_VD:(h + 1) * A_VD, :], h, m_ref, l_ref, acc_ref, init=init)

    @pl.when(first_ref[s_idx] == 1)
    def _():
        kpos = lax.broadcasted_iota(jnp.int32, (N_META, 1), 0)
        sweep(mk_ref, mvt_ref, kpos, init=True)

    def key_block(masked):
        kpos = N_META + kj_ref[s_idx] * TK + lax.broadcasted_iota(jnp.int32, (TK, 1), 0)
        sweep(k_ref, vt_ref, kpos if masked else None, init=False)

    pl.when(last_ref[s_idx] == 0)(functools.partial(key_block, False))
    pl.when(last_ref[s_idx] == 1)(functools.partial(key_block, True))

    @pl.when(last_ref[s_idx] == 1)
    def _():
        for h in range(A_HEADS):
            o = (acc_ref[h] / l_ref[h]).T
            y_ref[:, h * A_VD:(h + 1) * A_VD] = o.astype(BF16)


def _prompt_schedule():
    qi, kj, first, last, qpos = [], [], [], [], []
    for i in range(SEQ // TQ):
        nk = (i * TQ + TQ - 1) // TK + 1
        for j in range(nk):
            qi.append(i); kj.append(j); first.append(int(j == 0)); last.append(int(j == nk - 1))
            qpos.append(N_META + i * TQ)
    qi.append(ROW_M0 // TQ); kj.append(0); first.append(1); last.append(1); qpos.append(0)
    return [jnp.asarray(np.asarray(a, np.int32)) for a in (qi, kj, first, last, qpos)]


def _prompt_attn(qh, kh, vt, mvt, y_in):
    sched = _prompt_schedule()
    nsteps = int(sched[0].shape[0])
    mblk = ROW_M0 // N_META
    grid_spec = pltpu.PrefetchScalarGridSpec(
        num_scalar_prefetch=5,
        grid=(nsteps,),
        in_specs=[
            pl.BlockSpec((TQ, A_HEADS * HEAD_W), lambda s, qi, kj, f, l, p: (qi[s], 0)),
            pl.BlockSpec((TK, A_HEADS * HEAD_W), lambda s, qi, kj, f, l, p: (kj[s], 0)),
            pl.BlockSpec((A_HEADS * A_VD, TK), lambda s, qi, kj, f, l, p: (0, kj[s])),
            pl.BlockSpec((N_META, A_HEADS * HEAD_W), lambda s, qi, kj, f, l, p: (mblk, 0)),
            pl.BlockSpec(mvt.shape, lambda s, qi, kj, f, l, p: (0, 0)),
            pl.BlockSpec(memory_space=pl.ANY),
        ],
        out_specs=pl.BlockSpec((TQ, A_HEADS * A_VD), lambda s, qi, kj, f, l, p: (qi[s], 0)),
        scratch_shapes=[pltpu.VMEM((A_HEADS, 1, TQ), F32), pltpu.VMEM((A_HEADS, 1, TQ), F32),
                        pltpu.VMEM((A_HEADS, A_VD, TQ), F32)],
    )
    return pl.pallas_call(
        _prompt_attn_kernel,
        grid_spec=grid_spec,
        out_shape=jax.ShapeDtypeStruct(y_in.shape, BF16),
        input_output_aliases={5 + 5: 0},
        compiler_params=_cparams(("arbitrary",)),
        name="prompt_attn",
    )(*sched, qh, kh, vt, kh, mvt, y_in)


def _paged_attn_kernel(pt_ref, ql_ref, qr_ref, nl_ref, nr_ref, lat_hbm, rope_hbm, o_ref,
                       lat_buf, rope_buf, sem, qls_ref, qrs_ref, m_ref, l_ref, acc_ref):
    b = pl.program_id(0)
    nseq = pl.num_programs(0)
    ppc = PAGES_PER_STEP
    nch = (PAST_LEN // PAGE_SIZE) // ppc
    ahead = RING_SLOTS - 1

    def page_copies(seq, chunk, slot):
        cps = []
        for p in range(ppc):
            page = pt_ref[seq, chunk * ppc + p]
            cps.append(pltpu.make_async_copy(lat_hbm.at[page], lat_buf.at[slot, p], sem.at[0, slot]))
            cps.append(pltpu.make_async_copy(rope_hbm.at[page], rope_buf.at[slot, p], sem.at[1, slot]))
        return cps

    def start(seq, chunk, slot):
        for i, cp in enumerate(page_copies(seq, chunk, slot)):
            cp.start(priority=(i // 2) % 2)

    @pl.when(b == 0)
    def _():
        for c in range(ahead):
            start(0, c, c)

    qls_ref[...] = _stack_heads(ql_ref, KV_LORA)
    qrs_ref[...] = _stack_heads(qr_ref, ROPE_PAD)[:, :A_ROPE]
    m_ref[...] = jnp.full(m_ref.shape, -jnp.inf, F32)
    l_ref[...] = jnp.zeros(l_ref.shape, F32)
    acc_ref[...] = jnp.zeros(acc_ref.shape, F32)

    def update(g, s, vals):
        m_old = m_ref[g]
        m_new = jnp.maximum(m_old, jnp.max(s, axis=1, keepdims=True))
        alpha = jnp.exp2(m_old - m_new)
        p = jnp.exp2(s - m_new)
        m_ref[g] = m_new
        l_ref[g] = alpha * l_ref[g] + jnp.sum(p, axis=1, keepdims=True)
        acc_ref[g] = alpha * acc_ref[g] + jnp.dot(p.astype(BF16), vals, preferred_element_type=F32)

    def chunk_scores(chunk):
        slot = chunk % RING_SLOTS
        for cp in page_copies(b, chunk, slot):
            cp.wait()
        kl = jnp.concatenate([lat_buf[slot, p] for p in range(ppc)], axis=0).astype(BF16)
        krt = jnp.concatenate([rope_buf[slot, p] for p in range(ppc)], axis=1).astype(BF16)
        s = (lax.dot_general(qls_ref[...], kl, NT_DIMS, preferred_element_type=F32)
             + jnp.dot(qrs_ref[...], krt, preferred_element_type=F32)) * C_LOG2
        return s, kl

    cur = chunk_scores(0)
    for c in range(nch):
        n = c + ahead
        if n < nch:
            start(b, n, n % RING_SLOTS)
        else:
            pl.when(b + 1 < nseq)(functools.partial(start, b + 1, n - nch, n % RING_SLOTS))
        nxt = chunk_scores(c + 1) if c + 1 < nch else None
        update(c % KEY_GROUPS, *cur)
        cur = nxt

    def finalize():
        nl = nl_ref[...]
        s2 = (lax.dot_general(qls_ref[...], nl, NT_DIMS, preferred_element_type=F32)
              + lax.dot_general(qrs_ref[...], nr_ref[:, :A_ROPE], NT_DIMS,
                                preferred_element_type=F32)) * C_LOG2
        rows = A_HEADS * DEC_SEQ
        tq = lax.broadcasted_iota(jnp.int32, (rows, 1), 0) & (DEC_SEQ - 1)
        tk = lax.broadcasted_iota(jnp.int32, (1, DEC_SEQ), 1)
        update(0, jnp.where(tk <= tq, s2, -jnp.inf), nl)
        m_tot = m_ref[0]
        for g in range(1, KEY_GROUPS):
            m_tot = jnp.maximum(m_tot, m_ref[g])
        l_tot = jnp.zeros((rows, 1), F32)
        acc = jnp.zeros((rows, KV_LORA), F32)
        for g in range(KEY_GROUPS):
            w = jnp.exp2(m_ref[g] - m_tot)
            l_tot = l_tot + w * l_ref[g]
            acc = acc + w * acc_ref[g]
        o = acc / l_tot
        for h in range(A_HEADS):
            o_ref[:, h * KV_LORA:(h + 1) * KV_LORA] = o[h * DEC_SEQ:(h + 1) * DEC_SEQ, :].astype(BF16)

    finalize()


def _paged_attn(page_table, ql, qr, kl, kr, cache_lat, cache_rope_t):
    ppc = PAGES_PER_STEP
    rb0 = ROW_S0 // DEC_SEQ
    rows = A_HEADS * DEC_SEQ
    assert (PAST_LEN // PAGE_SIZE) % (RING_SLOTS * ppc) == 0
    rowblk = lambda width: pl.BlockSpec((DEC_SEQ, width), lambda b, pt: (rb0 + b, 0))
    grid_spec = pltpu.PrefetchScalarGridSpec(
        num_scalar_prefetch=1,
        grid=(DEC_BATCH,),
        in_specs=[rowblk(A_HEADS * KV_LORA), rowblk(A_HEADS * ROPE_PAD), rowblk(KV_LORA), rowblk(ROPE_PAD),
                  pl.BlockSpec(memory_space=pl.ANY), pl.BlockSpec(memory_space=pl.ANY)],
        out_specs=pl.BlockSpec((DEC_SEQ, A_HEADS * KV_LORA), lambda b, pt: (b, 0)),
        scratch_shapes=[pltpu.VMEM((RING_SLOTS, ppc, PAGE_SIZE, KV_LORA), F32),
                        pltpu.VMEM((RING_SLOTS, ppc, A_ROPE, PAGE_SIZE), F32),
                        pltpu.SemaphoreType.DMA((2, RING_SLOTS)),
                        pltpu.VMEM((rows, KV_LORA), BF16), pltpu.VMEM((rows, A_ROPE), BF16),
                        pltpu.VMEM((KEY_GROUPS, rows, 1), F32), pltpu.VMEM((KEY_GROUPS, rows, 1), F32),
                        pltpu.VMEM((KEY_GROUPS, rows, KV_LORA), F32)],
    )
    return pl.pallas_call(
        _paged_attn_kernel,
        grid_spec=grid_spec,
        out_shape=jax.ShapeDtypeStruct((DEC_BATCH * DEC_SEQ, A_HEADS * KV_LORA), BF16),
        compiler_params=_cparams(("arbitrary",)),
        name="paged_attn",
    )(page_table, ql, qr, kl, kr, cache_lat, cache_rope_t)


def _uvproj_kernel(o_ref, w_ref, yin_ref, y_ref):
    del yin_ref
    for h in range(A_HEADS):
        y = jnp.dot(o_ref[:, h * KV_LORA:(h + 1) * KV_LORA], w_ref[h], preferred_element_type=F32)
        y_ref[:, h * A_VD:(h + 1) * A_VD] = y.astype(BF16)


def _uvproj_sample(o, wuv, y_in):
    rb0 = ROW_S0 // TM
    return pl.pallas_call(
        _uvproj_kernel,
        grid=(o.shape[0] // TM,),
        in_specs=[pl.BlockSpec((TM, A_HEADS * KV_LORA), lambda i: (i, 0)),
                  pl.BlockSpec(wuv.shape, lambda i: (0, 0, 0)),
                  pl.BlockSpec(memory_space=pl.ANY)],
        out_specs=pl.BlockSpec((TM, A_HEADS * A_VD), lambda i: (rb0 + i, 0)),
        out_shape=jax.ShapeDtypeStruct(y_in.shape, BF16),
        input_output_aliases={2: 0},
        compiler_params=_cparams(("parallel",)),
        name="mla_uvproj_sample",
    )(o, wuv, y_in)


def _merge_kernel(ym_ref, ya_ref, wm_ref, wa_ref, gm_ref, ga_ref, t_ref):
    y_m = jnp.dot(ym_ref[...], wm_ref[...], preferred_element_type=F32)
    y_a = jnp.dot(ya_ref[...], wa_ref[...], preferred_element_type=F32)
    t_ref[...] = (gm_ref[...] * y_m + ga_ref[...] * y_a).astype(BF16)


def _merge(ym, ya, wm, wa, og):
    tn = 1024
    nb = D_MODEL // tn
    return pl.pallas_call(
        _merge_kernel,
        grid=(ROWS // TM, nb),
        in_specs=[pl.BlockSpec((TM, D_MODEL), lambda i, j: (i, 0)),
                  pl.BlockSpec((TM, D_MODEL), lambda i, j: (i, 0)),
                  pl.BlockSpec((D_MODEL, tn), lambda i, j: (0, j)),
                  pl.BlockSpec((D_MODEL, tn), lambda i, j: (0, j)),
                  pl.BlockSpec((TM, tn), lambda i, j: (i, nb + j)),
                  pl.BlockSpec((TM, tn), lambda i, j: (i, 2 * nb + j))],
        out_specs=pl.BlockSpec((TM, tn), lambda i, j: (i, j)),
        out_shape=jax.ShapeDtypeStruct((ROWS, D_MODEL), BF16),
        compiler_params=_cparams(("parallel", "arbitrary")),
        name="branch_merge",
    )(ym, ya, wm, wa, og, og)


def _layer_norm(z, g, b):
    mu = jnp.mean(z, axis=-1, keepdims=True)
    d = z - mu
    var = jnp.mean(d * d, axis=-1, keepdims=True)
    return d * lax.rsqrt(var + LN_EPS) * g + b


def _outln_kernel(t_ref, w_ref, x_ref, g_ref, b_ref, xf_ref, xb_ref):
    mix = jnp.dot(t_ref[...], w_ref[...], preferred_element_type=F32)
    x1 = _layer_norm(ALPHA * x_ref[...] + mix, g_ref[...], b_ref[...])
    xf_ref[...] = x1
    xb_ref[...] = x1.astype(BF16)


def _outln(t, w, x, g, b):
    row = pl.BlockSpec((TM, D_MODEL), lambda i: (i, 0))
    vec = pl.BlockSpec((1, D_MODEL), lambda i: (0, 0))
    sd = jax.ShapeDtypeStruct
    return pl.pallas_call(
        _outln_kernel,
        grid=(ROWS // TM,),
        in_specs=[row, pl.BlockSpec((D_MODEL, D_MODEL), lambda i: (0, 0)), row, vec, vec],
        out_specs=[row, row],
        out_shape=[sd((ROWS, D_MODEL), F32), sd((ROWS, D_MODEL), BF16)],
        compiler_params=_cparams(("parallel",)),
        name="out_proj_ln",
    )(t, w, x, g, b)


def _mlp_kernel(xb_ref, wu_ref, wd_ref, xf_ref, g_ref, b_ref, y_ref, acc_ref):
    f = pl.program_id(1)
    h = jnp.dot(xb_ref[...], wu_ref[...], preferred_element_type=F32)
    h = jnp.square(jnp.maximum(h, 0.0)).astype(BF16)
    part = jnp.dot(h, wd_ref[...], preferred_element_type=F32)

    @pl.when(f == 0)
    def _():
        acc_ref[...] = part

    @pl.when(f > 0)
    def _():
        acc_ref[...] += part

    @pl.when(f == pl.num_programs(1) - 1)
    def _():
        y_ref[...] = _layer_norm(ALPHA * xf_ref[...] + acc_ref[...], g_ref[...], b_ref[...])


def _mlp(xb, wu, wd, xf, g, b):
    tf = 1024
    row = pl.BlockSpec((TM, D_MODEL), lambda i, f: (i, 0))
    vec = pl.BlockSpec((1, D_MODEL), lambda i, f: (0, 0))
    return pl.pallas_call(
        _mlp_kernel,
        grid=(ROWS // TM, D_FF // tf),
        in_specs=[row, pl.BlockSpec((D_MODEL, tf), lambda i, f: (0, f)),
                  pl.BlockSpec((tf, D_MODEL), lambda i, f: (f, 0)), row, vec, vec],
        out_specs=row,
        out_shape=jax.ShapeDtypeStruct((ROWS, D_MODEL), F32),
        scratch_shapes=[pltpu.VMEM((TM, D_MODEL), F32)],
        compiler_params=_cparams(("parallel", "arbitrary")),
        name="mlp_ln",
    )(xb, wu, wd, xf, g, b)


def _rot_cols(w):
    half = A_ROPE // 2
    return jnp.concatenate([-w[..., half:], w[..., :half]], axis=-1)


def _pad_cols(w, n):
    return jnp.pad(w, [(0, 0)] * (w.ndim - 1) + [(0, n - w.shape[-1])])


def _rope_tables():
    half = A_ROPE // 2
    pos = jnp.concatenate([
        N_META + jnp.arange(SEQ, dtype=jnp.int32),
        jnp.tile(PAST_LEN + jnp.arange(DEC_SEQ, dtype=jnp.int32), DEC_BATCH),
        jnp.arange(N_META, dtype=jnp.int32),
        jnp.zeros((ROWS - ROW_M0 - N_META,), jnp.int32)])
    inv = ROPE_BASE ** (-jnp.arange(half, dtype=F32) / half)
    ang = pos.astype(F32)[:, None] * inv
    cos, sin = jnp.cos(ang), jnp.sin(ang)
    return (_pad_cols(jnp.concatenate([cos, cos], axis=-1), ROPE_PAD),
            _pad_cols(jnp.concatenate([sin, sin], axis=-1), ROPE_PAD))


def kernel(x_prompt, x_sample, cache_latent, cache_krope, state_C, state_n, state_m, page_table, meta, w_in, b_in, mh_g, q_norm_g, kv_norm_g, w_uq, w_uk, w_uv, w_br_m, w_br_a, w_out, ln1_g, ln1_b, w_up, w_down, ln2_g, ln2_b):
    wi, bi = w_in[0], b_in[0]
    x_all = jnp.concatenate([
        x_prompt[0], x_sample.reshape(DEC_BATCH * DEC_SEQ, D_MODEL), meta,
        jnp.zeros((ROWS - ROW_M0 - N_META, D_MODEL), F32)], axis=0)
    cos_t, sin_t = _rope_tables()

    ones = lambda n: jnp.ones((1, n), F32)
    k_scale = jnp.concatenate([jnp.ones((1, 1024), F32), jnp.full((1, 1024), M_DK ** -0.5, F32),
                               jnp.ones((1, 2048), F32)], axis=1)
    qkv = _matmul(x_all, wi[:, OFF_Q:OFF_O].astype(BF16), bi[None, OFF_Q:OFF_O], k_scale,
                  act=None, out_dtype=BF16, tn=2048, name="inproj_qkv")
    w_og = jnp.concatenate([wi[:, OFF_O:OFF_I], wi[:, OFF_G:]], axis=1).astype(BF16)
    b_og = jnp.concatenate([bi[OFF_O:OFF_I], bi[OFF_G:]])[None]
    og = _matmul(x_all, w_og, b_og, ones(w_og.shape[1]), act="sigmoid", out_dtype=F32, tn=2048,
                 name="inproj_gates")

    def tail_cols(a):
        kr = a[..., OFF_KR:OFF_G]
        return jnp.concatenate([
            a[..., OFF_AQ:OFF_KR], _pad_cols(kr, ROPE_PAD), _pad_cols(_rot_cols(kr), ROPE_PAD),
            _pad_cols(a[..., OFF_I:OFF_AQ], LANE)], axis=-1)

    qn, ckv_f, ckv_b, kr_f, kr_b, gates = _tail(
        x_all, tail_cols(wi).astype(BF16), tail_cols(bi)[None], q_norm_g, kv_norm_g, cos_t, sin_t)

    mhg = mh_g
    ym = jnp.zeros((ROWS, M_HEADS * M_DV), BF16)
    zc = jnp.zeros((1, M_HEADS, M_DK, M_DV), F32)
    zn = jnp.zeros((1, M_HEADS, M_DK), F32)
    zm = jnp.zeros((1, M_HEADS, 1), F32)
    ym, c_m, n_m, m_m = _mlstm(qkv, gates, og, mhg, zc, zn, zm, ym, length=N_META, nseq=1, nsteps=1,
                               row0=ROW_M0, carry=True, name="mlstm_meta")
    ym, p_c, p_n, p_m = _mlstm(qkv, gates, og, mhg, c_m, n_m, m_m, ym, length=M_CHUNK_P, nseq=1,
                               nsteps=SEQ // M_CHUNK_P, row0=0, carry=True, name="mlstm_prompt")
    ym, s_c, s_n, s_m = _mlstm(qkv, gates, og, mhg, state_C[0], state_n[0], state_m[0][..., None], ym,
                               length=DEC_SEQ, nseq=SEQ_PER_STEP, nsteps=DEC_BATCH // SEQ_PER_STEP,
                               row0=ROW_S0, carry=False, name="mlstm_sample")

    wq = w_uq[0].reshape(Q_LORA, A_HEADS, A_NOPE + A_ROPE).transpose(1, 0, 2)
    wq_r = wq[..., A_NOPE:]
    wq_p = jnp.concatenate([wq[..., :A_NOPE], _pad_cols(wq_r, ROPE_PAD), _pad_cols(_rot_cols(wq_r), ROPE_PAD)],
                           axis=-1).astype(BF16)
    wuk = w_uk[0].transpose(1, 2, 0).astype(BF16)
    wuv = w_uv[0].transpose(1, 0, 2).astype(BF16)
    q_lat, q_rope, q_heads = _qproj(qn, wq_p, wuk, cos_t, sin_t)
    wk = w_uk[0].transpose(1, 0, 2).astype(BF16)
    wvt = w_uv[0].transpose(1, 2, 0).astype(BF16)
    k_heads, v_heads_t = _kvproj(ckv_b, kr_b, wk, wvt)
    ya = jnp.zeros((ROWS, A_HEADS * A_VD), BF16)
    ya = _prompt_attn(q_heads, k_heads, v_heads_t, v_heads_t[:, ROW_M0:ROW_M0 + N_META], ya)
    o_s = _paged_attn(page_table, q_lat, q_rope, ckv_b, kr_b, cache_latent[0],
                      jnp.swapaxes(cache_krope[0], 1, 2))
    ya = _uvproj_sample(o_s, wuv, ya)

    t = _merge(ym, ya, w_br_m[0].astype(BF16), w_br_a[0].astype(BF16), og)
    x1f, x1b = _outln(t, w_out[0].astype(BF16), x_all, ln1_g, ln1_b)
    y = _mlp(x1b, w_up[0].astype(BF16), w_down[0].astype(BF16), x1f, ln2_g, ln2_b)

    def prompt_rows(a, width):
        return jnp.concatenate([a[ROW_M0:ROW_M0 + N_META, :width], a[:SEQ, :width]], axis=0)[None, None]

    def sample_rows(a, width):
        return a[ROW_S0:ROW_M0, :width].reshape(1, DEC_BATCH, DEC_SEQ, width)

    return (y[:SEQ][None], y[ROW_S0:ROW_M0].reshape(DEC_BATCH, DEC_SEQ, D_MODEL),
            prompt_rows(ckv_f, KV_LORA), prompt_rows(kr_f, A_ROPE),
            p_c[None], p_n[None], p_m[..., 0][None],
            sample_rows(ckv_f, KV_LORA), sample_rows(kr_f, A_ROPE),
            s_c[None], s_n[None], s_m[..., 0][None])
```

```python
import functools
import math

import numpy as np
import jax
import jax.numpy as jnp
from jax import lax
from jax.experimental import pallas as pl
from jax.experimental.pallas import tpu as pltpu

F32 = jnp.float32
BF16 = jnp.bfloat16

D_MODEL = 2048
SEQ = 8192
DEC_BATCH = 128
DEC_SEQ = 8
PAST_LEN = 16384
PAGE_SIZE = 128
N_META = 16
M_HEADS = 8
M_DK = 128
M_DV = 256
A_HEADS = 16
A_NOPE = 128
A_ROPE = 64
A_VD = 128
Q_LORA = 512
KV_LORA = 256
ROPE_BASE = 10000.0
ATTN_SCALE = (A_NOPE + A_ROPE) ** -0.5
D_FF = 4 * D_MODEL
ALPHA = 2.0 ** 0.25
LN_EPS = 1e-5
RMS_EPS = 1e-6

OFF_Q, OFF_K, OFF_V, OFF_O, OFF_I, OFF_F, OFF_AQ, OFF_CKV, OFF_KR, OFF_G = (
    0, 1024, 2048, 4096, 6144, 6152, 6160, 6672, 6928, 6992)
D_IN = 11088

ROW_S0 = SEQ
ROW_M0 = SEQ + DEC_BATCH * DEC_SEQ
ROWS = 9728
TM = 512
LANE = 128
ROPE_PAD = LANE

M_CHUNK_P = 256
SEQ_PER_STEP = 2
TQ = 256
TK = 512
PAGES_PER_STEP = 8
RING_SLOTS = 4
KEY_GROUPS = 2
PAGED_LOOKAHEAD = 1
VMEM_LIMIT = 56 * 1024 * 1024


def _cparams(sem):
    return pltpu.CompilerParams(dimension_semantics=sem, vmem_limit_bytes=VMEM_LIMIT)


PROMPT_TILES = SEQ // TM


def _x_specs():
    return [pl.BlockSpec((TM, D_MODEL), lambda i, *_: (jnp.minimum(i, PROMPT_TILES - 1), 0)),
            pl.BlockSpec((TM, D_MODEL), lambda i, *_: (jnp.maximum(i - PROMPT_TILES, 0), 0))]


def _x_tile(xp_ref, xr_ref):
    return jnp.where(pl.program_id(0) < PROMPT_TILES, xp_ref[...], xr_ref[...])


def _mm_kernel(xp_ref, xr_ref, w_ref, b_ref, s_ref, o_ref, xb_ref, *, act):
    is_prompt = pl.program_id(0) < PROMPT_TILES

    @pl.when(pl.program_id(1) == 0)
    def _():
        xb_ref[...] = jnp.where(is_prompt, xp_ref[...], xr_ref[...]).astype(BF16)

    y = jnp.dot(xb_ref[...], w_ref[...], preferred_element_type=F32)
    y = (y + b_ref[...]) * s_ref[...]
    if act == "sigmoid":
        y = jax.nn.sigmoid(y)
    o_ref[...] = y.astype(o_ref.dtype)


def _matmul(xp, xr, w, b, s, *, act, out_dtype, tn, name):
    k, n = w.shape
    return pl.pallas_call(
        functools.partial(_mm_kernel, act=act),
        grid=(ROWS // TM, n // tn),
        in_specs=_x_specs() + [
            pl.BlockSpec((k, tn), lambda i, j: (0, j)),
            pl.BlockSpec((1, tn), lambda i, j: (0, j)),
            pl.BlockSpec((1, tn), lambda i, j: (0, j)),
        ],
        out_specs=pl.BlockSpec((TM, tn), lambda i, j: (i, j)),
        out_shape=jax.ShapeDtypeStruct((ROWS, n), out_dtype),
        scratch_shapes=[pltpu.VMEM((TM, k), BF16)],
        compiler_params=_cparams(("parallel", "arbitrary")),
        name=name,
    )(xp, xr, w, b, s)


T_AQ, T_CKV, T_KR, T_KROT, T_G, T_END = 0, 512, 768, 896, 1024, 1152


def _tail_kernel(xp_ref, xr_ref, w_ref, b_ref, qg_ref, kg_ref, cos_ref, sin_ref,
                 qn_ref, cf_ref, cb_ref, rf_ref, rb_ref, g_ref):
    xb = _x_tile(xp_ref, xr_ref).astype(BF16)
    acc = jnp.dot(xb, w_ref[...], preferred_element_type=F32) + b_ref[...]
    aq = acc[:, T_AQ:T_CKV]
    qn = aq * lax.rsqrt(jnp.mean(aq * aq, axis=-1, keepdims=True) + RMS_EPS) * qg_ref[...]
    qn_ref[...] = qn.astype(BF16)
    ckv = acc[:, T_CKV:T_KR]
    c = ckv * lax.rsqrt(jnp.mean(ckv * ckv, axis=-1, keepdims=True) + RMS_EPS) * kg_ref[...]
    cf_ref[...] = c
    cb_ref[...] = c.astype(BF16)
    kr = acc[:, T_KR:T_KROT] * cos_ref[...] + acc[:, T_KROT:T_G] * sin_ref[...]
    rf_ref[...] = kr
    rb_ref[...] = kr.astype(BF16)
    g_ref[...] = acc[:, T_G:T_END]


def _tail(xp, xr, w, b, qg, kg, cos, sin):
    row = lambda n: pl.BlockSpec((TM, n), lambda i: (i, 0))
    full = lambda a: pl.BlockSpec(a.shape, lambda i: (0,) * a.ndim)
    sd = jax.ShapeDtypeStruct
    return pl.pallas_call(
        _tail_kernel,
        grid=(ROWS // TM,),
        in_specs=_x_specs() + [full(w), full(b), full(qg), full(kg), row(ROPE_PAD), row(ROPE_PAD)],
        out_specs=[row(Q_LORA), row(KV_LORA), row(KV_LORA), row(ROPE_PAD), row(ROPE_PAD), row(LANE)],
        out_shape=[sd((ROWS, Q_LORA), BF16), sd((ROWS, KV_LORA), F32), sd((ROWS, KV_LORA), BF16),
                   sd((ROWS, ROPE_PAD), F32), sd((ROWS, ROPE_PAD), BF16), sd((ROWS, LANE), F32)],
        compiler_params=_cparams(("parallel",)),
        name="inproj_tail",
    )(xp, xr, w, b, qg, kg, cos, sin)


def _log_sigmoid(x):
    return jnp.minimum(x, 0.0) - jnp.log1p(jnp.exp(-jnp.abs(x)))


def _mlstm_head(q, k, v, ig_c, ig_r, lf_c, lf_r, c0, n0, m0, length):
    ti = lax.broadcasted_iota(jnp.int32, (length, length), 0)
    si = lax.broadcasted_iota(jnp.int32, (length, length), 1)
    causal = si <= ti
    b_c = jnp.sum(jnp.where(causal, lf_r, 0.0), axis=1, keepdims=True)
    b_r = jnp.sum(jnp.where(ti <= si, lf_c, 0.0), axis=0, keepdims=True)
    dlog = jnp.where(causal, b_c - b_r + ig_r, -jnp.inf)
    m = jnp.maximum(b_c + m0, jnp.max(dlog, axis=1, keepdims=True))
    w = jnp.exp(dlog - m)
    inter = jnp.exp(b_c + m0 - m)
    s = lax.dot_general(q, k, (((1,), (1,)), ((), ())), preferred_element_type=F32) * w
    qc = jnp.dot(q, c0.astype(BF16), preferred_element_type=F32)
    num = jnp.dot(s.astype(BF16), v, preferred_element_type=F32) + inter * qc
    qn = jnp.sum(q.astype(F32) * n0, axis=1, keepdims=True)
    nq = jnp.sum(s, axis=1, keepdims=True) + inter * qn
    h = num / jnp.maximum(jnp.abs(nq), jnp.exp(-m))
    m_end = m[length - 1:length, :]
    b_end = b_c[length - 1:length, :]
    w_end = jnp.exp(b_end - b_c + ig_c - m_end)
    dec = jnp.exp(b_end + m0 - m_end)
    kw = k.astype(F32) * w_end
    c = dec * c0 + lax.dot_general(kw.astype(BF16), v, (((0,), (0,)), ((), ())),
                                   preferred_element_type=F32)
    n = dec * n0 + jnp.sum(kw, axis=0, keepdims=True)
    return h, c, n, m_end


def _mlstm_kernel(q_ref, k_ref, v_ref, g_ref, og_ref, mhg_ref, c0_ref, n0_ref, m0_ref, yin_ref,
                  y_ref, c_ref, n_ref, m_ref, cs_ref, ns_ref, ms_ref, *, length, nseq, carry):
    del yin_ref
    step = pl.program_id(0)
    if carry:
        @pl.when(step == 0)
        def _():
            cs_ref[...] = c0_ref[...]
            ns_ref[...] = n0_ref[...]
            ms_ref[...] = m0_ref[...]
        c_src, n_src, m_src = cs_ref, ns_ref, ms_ref
    else:
        c_src, n_src, m_src = c0_ref, n0_ref, m0_ref

    def head_stages(b, h, g, gt, lfc_all, lfr_all):
        r0 = b * length
        q = q_ref[r0:r0 + length, h * M_DK:(h + 1) * M_DK]
        k = k_ref[r0:r0 + length, h * M_DK:(h + 1) * M_DK]
        v = v_ref[r0:r0 + length, h * M_DV:(h + 1) * M_DV]
        return _mlstm_head(
            q, k, v,
            g[:, h:h + 1], gt[h:h + 1, :],
            lfc_all[:, M_HEADS + h:M_HEADS + h + 1], lfr_all[M_HEADS + h:M_HEADS + h + 1, :],
            c_src[b, h], n_src[b, h:h + 1, :], m_src[b, h:h + 1, :], length)

    def readout(b, h, result):
        hh, c, n, m_end = result
        r0 = b * length
        mu = jnp.mean(hh, axis=1, keepdims=True)
        d = hh - mu
        var = jnp.mean(d * d, axis=1, keepdims=True)
        hn = d * lax.rsqrt(var + LN_EPS)
        cols = slice(h * M_DV, (h + 1) * M_DV)
        y_ref[r0:r0 + length, cols] = (hn * mhg_ref[:, cols] * og_ref[r0:r0 + length, cols]).astype(BF16)
        c_dst, n_dst, m_dst = (cs_ref, ns_ref, ms_ref) if carry else (c_ref, n_ref, m_ref)
        c_dst[b, h] = c
        n_dst[b, h:h + 1, :] = n
        m_dst[b, h:h + 1, :] = m_end

    for b in range(nseq):
        r0 = b * length
        g = g_ref[r0:r0 + length, :]
        gt = g.T
        lfc_all = _log_sigmoid(g)
        lfr_all = _log_sigmoid(gt)
        for h in range(M_HEADS):
            readout(b, h, head_stages(b, h, g, gt, lfc_all, lfr_all))

    if carry:
        @pl.when(step == pl.num_programs(0) - 1)
        def _():
            c_ref[...] = cs_ref[...]
            n_ref[...] = ns_ref[...]
            m_ref[...] = ms_ref[...]


def _mlstm(qkv, gates, og, mhg, c0, n0, m0, y_in, *, length, nseq, nsteps, row0, carry, name):
    rows = nseq * length
    rb0 = row0 // rows
    nstate = c0.shape[0]
    if carry:
        st = lambda *tail: pl.BlockSpec((nstate,) + tail, lambda s: (0,) * (1 + len(tail)))
    else:
        st = lambda *tail: pl.BlockSpec((nseq,) + tail, lambda s: (s,) + (0,) * len(tail))
    sd = jax.ShapeDtypeStruct
    return pl.pallas_call(
        functools.partial(_mlstm_kernel, length=length, nseq=nseq, carry=carry),
        grid=(nsteps,),
        in_specs=[
            pl.BlockSpec((rows, M_HEADS * M_DK), lambda s: (rb0 + s, 0)),
            pl.BlockSpec((rows, M_HEADS * M_DK), lambda s: (rb0 + s, 1)),
            pl.BlockSpec((rows, M_HEADS * M_DV), lambda s: (rb0 + s, 1)),
            pl.BlockSpec((rows, LANE), lambda s: (rb0 + s, 0)),
            pl.BlockSpec((rows, M_HEADS * M_DV), lambda s: (rb0 + s, 0)),
            pl.BlockSpec((1, M_HEADS * M_DV), lambda s: (0, 0)),
            st(M_HEADS, M_DK, M_DV), st(M_HEADS, M_DK), st(M_HEADS, 1),
            pl.BlockSpec(memory_space=pl.ANY),
        ],
        out_specs=[
            pl.BlockSpec((rows, M_HEADS * M_DV), lambda s: (rb0 + s, 0)),
            st(M_HEADS, M_DK, M_DV), st(M_HEADS, M_DK), st(M_HEADS, 1),
        ],
        out_shape=[sd(y_in.shape, BF16), sd(c0.shape, F32), sd(n0.shape, F32), sd(m0.shape, F32)],
        scratch_shapes=[pltpu.VMEM((nstate, M_HEADS, M_DK, M_DV), F32),
                        pltpu.VMEM((nstate, M_HEADS, M_DK), F32),
                        pltpu.VMEM((nstate, M_HEADS, 1), F32)] if carry else
                       [pltpu.VMEM((8, LANE), F32)] * 3,
        input_output_aliases={9: 0},
        compiler_params=_cparams(("arbitrary",)),
        name=name,
    )(qkv, qkv, qkv, gates, og, mhg, c0, n0, m0, y_in)


QW = A_NOPE + 2 * ROPE_PAD


HEAD_W = A_NOPE + ROPE_PAD


def _qproj_kernel(qn_ref, wq_ref, wuk_ref, cos_ref, sin_ref, ql_ref, qr_ref, qh_ref):
    qn = qn_ref[...]
    for h in range(A_HEADS):
        q = jnp.dot(qn, wq_ref[h], preferred_element_type=F32)
        q_nope = q[:, :A_NOPE].astype(BF16)
        ql = jnp.dot(q_nope, wuk_ref[h], preferred_element_type=F32)
        ql_ref[:, h * KV_LORA:(h + 1) * KV_LORA] = ql.astype(BF16)
        qr_f = q[:, A_NOPE:A_NOPE + ROPE_PAD] * cos_ref[...] + q[:, A_NOPE + ROPE_PAD:] * sin_ref[...]
        qr_ref[:, h * ROPE_PAD:(h + 1) * ROPE_PAD] = qr_f.astype(BF16)
        qh_ref[:, h * HEAD_W:h * HEAD_W + A_NOPE] = (q[:, :A_NOPE] * C_LOG2).astype(BF16)
        qh_ref[:, h * HEAD_W + A_NOPE:(h + 1) * HEAD_W] = (qr_f * C_LOG2).astype(BF16)


def _qproj(qn, wq, wuk, cos, sin):
    sd = jax.ShapeDtypeStruct
    row = lambda n: pl.BlockSpec((TM, n), lambda i: (i, 0))
    full = lambda a: pl.BlockSpec(a.shape, lambda i: (0,) * a.ndim)
    return pl.pallas_call(
        _qproj_kernel,
        grid=(ROWS // TM,),
        in_specs=[row(Q_LORA), full(wq), full(wuk), row(ROPE_PAD), row(ROPE_PAD)],
        out_specs=[row(A_HEADS * KV_LORA), row(A_HEADS * ROPE_PAD), row(A_HEADS * HEAD_W)],
        out_shape=[sd((ROWS, A_HEADS * KV_LORA), BF16), sd((ROWS, A_HEADS * ROPE_PAD), BF16),
                   sd((ROWS, A_HEADS * HEAD_W), BF16)],
        compiler_params=_cparams(("parallel",)),
        name="mla_qproj",
    )(qn, wq, wuk, cos, sin)


def _kvproj_kernel(c_ref, kr_ref, wk_ref, wvt_ref, k_ref, vt_ref):
    c = c_ref[...]
    kr = kr_ref[...]
    for h in range(A_HEADS):
        kn = jnp.dot(c, wk_ref[h], preferred_element_type=F32)
        k_ref[:, h * HEAD_W:h * HEAD_W + A_NOPE] = kn.astype(BF16)
        k_ref[:, h * HEAD_W + A_NOPE:(h + 1) * HEAD_W] = kr
        vt = lax.dot_general(wvt_ref[h], c, NT_DIMS, preferred_element_type=F32)
        vt_ref[h * A_VD:(h + 1) * A_VD, :] = vt.astype(BF16)


def _kvproj(c, kr, wk, wvt):
    sd = jax.ShapeDtypeStruct
    row = lambda n: pl.BlockSpec((TM, n), lambda i: (i, 0))
    full = lambda a: pl.BlockSpec(a.shape, lambda i: (0,) * a.ndim)
    return pl.pallas_call(
        _kvproj_kernel,
        grid=(ROWS // TM,),
        in_specs=[row(KV_LORA), row(ROPE_PAD), full(wk), full(wvt)],
        out_specs=[row(A_HEADS * HEAD_W), pl.BlockSpec((A_HEADS * A_VD, TM), lambda i: (0, i))],
        out_shape=[sd((ROWS, A_HEADS * HEAD_W), BF16), sd((A_HEADS * A_VD, ROWS), BF16)],
        compiler_params=_cparams(("parallel",)),
        name="mla_kvproj",
    )(c, kr, wk, wvt)


def _stack_heads(ref, width):
    return jnp.concatenate([ref[:, h * width:(h + 1) * width] for h in range(A_HEADS)], axis=0)


NT_DIMS = (((1,), (1,)), ((), ()))
C_LOG2 = ATTN_SCALE * math.log2(math.e)
SCORE_LOOKAHEAD = 3


def _softmax_update_t(st, vt, g, m_ref, l_ref, acc_ref, init):
    mx = jnp.max(st, axis=0, keepdims=True)
    if init:
        m_new = mx
        p = jnp.exp2(st - m_new)
        l_ref[g] = jnp.sum(p, axis=0, keepdims=True)
        acc_ref[g] = jnp.dot(vt, p.astype(BF16), preferred_element_type=F32)
    else:
        m_old = m_ref[g]
        m_new = jnp.maximum(m_old, mx)
        alpha = jnp.exp2(m_old - m_new)
        p = jnp.exp2(st - m_new)
        l_ref[g] = alpha * l_ref[g] + jnp.sum(p, axis=0, keepdims=True)
        acc_ref[g] = alpha * acc_ref[g] + jnp.dot(vt, p.astype(BF16), preferred_element_type=F32)
    m_ref[g] = m_new


def _prompt_attn_kernel(qi_ref, kj_ref, first_ref, last_ref, qpos_ref,
                        q_ref, k_ref, vt_ref, mk_ref, mvt_ref, yin_ref,
                        y_ref, m_ref, l_ref, acc_ref):
    del yin_ref
    s_idx = pl.program_id(0)
    qpos = qpos_ref[s_idx] + lax.broadcasted_iota(jnp.int32, (1, TQ), 1)

    def scores(keys_ref, h):
        cols = slice(h * HEAD_W, (h + 1) * HEAD_W)
        return lax.dot_general(keys_ref[:, cols], q_ref[:, cols], NT_DIMS,
                               preferred_element_type=F32)

    def sweep(keys_ref, v_t_ref, kpos, init):
        ahead = [scores(keys_ref, h) for h in range(SCORE_LOOKAHEAD)]
        for h in range(A_HEADS):
            st = ahead.pop(0)
            if h + SCORE_LOOKAHEAD < A_HEADS:
                ahead.append(scores(keys_ref, h + SCORE_LOOKAHEAD))
            if kpos is not None:
                st = jnp.where(kpos <= qpos, st, -jnp.inf)
            _softmax_update_t(st, v_t_ref[h * A_VD:(h + 1) * A_VD, :], h, m_ref, l_ref, acc_ref, init=init)

    @pl.when(first_ref[s_idx] == 1)
    def _():
        kpos = lax.broadcasted_iota(jnp.int32, (N_META, 1), 0)
        sweep(mk_ref, mvt_ref, kpos, init=True)

    def key_block(masked):
        kpos = N_META + kj_ref[s_idx] * TK + lax.broadcasted_iota(jnp.int32, (TK, 1), 0)
        sweep(k_ref, vt_ref, kpos if masked else None, init=False)

    pl.when(last_ref[s_idx] == 0)(functools.partial(key_block, False))
    pl.when(last_ref[s_idx] == 1)(functools.partial(key_block, True))

    @pl.when(last_ref[s_idx] == 1)
    def _():
        for h in range(A_HEADS):
            o = (acc_ref[h] / l_ref[h]).T
            y_ref[:, h * A_VD:(h + 1) * A_VD] = o.astype(BF16)


def _prompt_schedule():
    qi, kj, first, last, qpos = [], [], [], [], []
    for i in range(SEQ // TQ):
        nk = (i * TQ + TQ - 1) // TK + 1
        for j in range(nk):
            qi.append(i); kj.append(j); first.append(int(j == 0)); last.append(int(j == nk - 1))
            qpos.append(N_META + i * TQ)
    qi.append(ROW_M0 // TQ); kj.append(0); first.append(1); last.append(1); qpos.append(0)
    return [jnp.asarray(np.asarray(a, np.int32)) for a in (qi, kj, first, last, qpos)]


def _prompt_attn(qh, kh, vt, mvt, y_in):
    sched = _prompt_schedule()
    nsteps = int(sched[0].shape[0])
    mblk = ROW_M0 // N_META
    grid_spec = pltpu.PrefetchScalarGridSpec(
        num_scalar_prefetch=5,
        grid=(nsteps,),
        in_specs=[
            pl.BlockSpec((TQ, A_HEADS * HEAD_W), lambda s, qi, kj, f, l, p: (qi[s], 0)),
            pl.BlockSpec((TK, A_HEADS * HEAD_W), lambda s, qi, kj, f, l, p: (kj[s], 0)),
            pl.BlockSpec((A_HEADS * A_VD, TK), lambda s, qi, kj, f, l, p: (0, kj[s])),
            pl.BlockSpec((N_META, A_HEADS * HEAD_W), lambda s, qi, kj, f, l, p: (mblk, 0)),
            pl.BlockSpec(mvt.shape, lambda s, qi, kj, f, l, p: (0, 0)),
            pl.BlockSpec(memory_space=pl.ANY),
        ],
        out_specs=pl.BlockSpec((TQ, A_HEADS * A_VD), lambda s, qi, kj, f, l, p: (qi[s], 0)),
        scratch_shapes=[pltpu.VMEM((A_HEADS, 1, TQ), F32), pltpu.VMEM((A_HEADS, 1, TQ), F32),
                        pltpu.VMEM((A_HEADS, A_VD, TQ), F32)],
    )
    return pl.pallas_call(
        _prompt_attn_kernel,
        grid_spec=grid_spec,
        out_shape=jax.ShapeDtypeStruct(y_in.shape, BF16),
        input_output_aliases={5 + 5: 0},
        compiler_params=_cparams(("arbitrary",)),
        name="prompt_attn",
    )(*sched, qh, kh, vt, kh, mvt, y_in)


def _paged_attn_kernel(pt_ref, ql_ref, qr_ref, nl_ref, nr_ref, lat_hbm, rope_hbm, o_ref,
                       lat_buf, rope_buf, sem, qls_ref, qrs_ref, m_ref, l_ref, acc_ref):
    b = pl.program_id(0)
    nseq = pl.num_programs(0)
    ppc = PAGES_PER_STEP
    nch = (PAST_LEN // PAGE_SIZE) // ppc
    ahead = RING_SLOTS - 1

    def page_copies(seq, chunk, slot):
        cps = []
        for p in range(ppc):
            page = pt_ref[seq, chunk * ppc + p]
            cps.append(pltpu.make_async_copy(lat_hbm.at[page], lat_buf.at[slot, p], sem.at[0, slot]))
            cps.append(pltpu.make_async_copy(rope_hbm.at[page], rope_buf.at[slot, p], sem.at[1, slot]))
        return cps

    def start(seq, chunk, slot):
        for cp in page_copies(seq, chunk, slot):
            cp.start()

    @pl.when(b == 0)
    def _():
        for c in range(ahead):
            start(0, c, c)

    qls_ref[...] = _stack_heads(ql_ref, KV_LORA)
    qrs_ref[...] = _stack_heads(qr_ref, ROPE_PAD)[:, :A_ROPE]
    m_ref[...] = jnp.full(m_ref.shape, -jnp.inf, F32)
    l_ref[...] = jnp.zeros(l_ref.shape, F32)
    acc_ref[...] = jnp.zeros(acc_ref.shape, F32)

    def update(g, s, vals):
        m_old = m_ref[g]
        m_new = jnp.maximum(m_old, jnp.max(s, axis=1, keepdims=True))
        alpha = jnp.exp2(m_old - m_new)
        p = jnp.exp2(s - m_new)
        m_ref[g] = m_new
        l_ref[g] = alpha * l_ref[g] + jnp.sum(p, axis=1, keepdims=True)
        acc_ref[g] = alpha * acc_ref[g] + jnp.dot(p.astype(BF16), vals, preferred_element_type=F32)

    def chunk_scores(chunk):
        slot = chunk % RING_SLOTS
        for cp in page_copies(b, chunk, slot):
            cp.wait()
        kl = jnp.concatenate([lat_buf[slot, p] for p in range(ppc)], axis=0).astype(BF16)
        krt = jnp.concatenate([rope_buf[slot, p] for p in range(ppc)], axis=1).astype(BF16)
        s_lat_t = lax.dot_general(kl, qls_ref[...], NT_DIMS, preferred_element_type=F32)
        s = (s_lat_t.T + jnp.dot(qrs_ref[...], krt, preferred_element_type=F32)) * C_LOG2
        return s, kl

    pending = [chunk_scores(c) for c in range(PAGED_LOOKAHEAD)]
    for c in range(nch):
        n = c + ahead
        if n < nch:
            start(b, n, n % RING_SLOTS)
        else:
            pl.when(b + 1 < nseq)(functools.partial(start, b + 1, n - nch, n % RING_SLOTS))
        if c + PAGED_LOOKAHEAD < nch:
            pending.append(chunk_scores(c + PAGED_LOOKAHEAD))
        update(c % KEY_GROUPS, *pending.pop(0))

    def finalize():
        nl = nl_ref[...]
        s2 = (lax.dot_general(qls_ref[...], nl, NT_DIMS, preferred_element_type=F32)
              + lax.dot_general(qrs_ref[...], nr_ref[:, :A_ROPE], NT_DIMS,
                                preferred_element_type=F32)) * C_LOG2
        rows = A_HEADS * DEC_SEQ
        tq = lax.broadcasted_iota(jnp.int32, (rows, 1), 0) & (DEC_SEQ - 1)
        tk = lax.broadcasted_iota(jnp.int32, (1, DEC_SEQ), 1)
        update(0, jnp.where(tk <= tq, s2, -jnp.inf), nl)
        m_tot = m_ref[0]
        for g in range(1, KEY_GROUPS):
            m_tot = jnp.maximum(m_tot, m_ref[g])
        l_tot = jnp.zeros((rows, 1), F32)
        acc = jnp.zeros((rows, KV_LORA), F32)
        for g in range(KEY_GROUPS):
            w = jnp.exp2(m_ref[g] - m_tot)
            l_tot = l_tot + w * l_ref[g]
            acc = acc + w * acc_ref[g]
        o = acc / l_tot
        for h in range(A_HEADS):
            o_ref[:, h * KV_LORA:(h + 1) * KV_LORA] = o[h * DEC_SEQ:(h + 1) * DEC_SEQ, :].astype(BF16)

    finalize()


def _paged_attn(page_table, ql, qr, kl, kr, cache_lat, cache_rope_t):
    ppc = PAGES_PER_STEP
    rb0 = ROW_S0 // DEC_SEQ
    rows = A_HEADS * DEC_SEQ
    assert (PAST_LEN // PAGE_SIZE) % (RING_SLOTS * ppc) == 0
    rowblk = lambda width: pl.BlockSpec((DEC_SEQ, width), lambda b, pt: (rb0 + b, 0))
    grid_spec = pltpu.PrefetchScalarGridSpec(
        num_scalar_prefetch=1,
        grid=(DEC_BATCH,),
        in_specs=[rowblk(A_HEADS * KV_LORA), rowblk(A_HEADS * ROPE_PAD), rowblk(KV_LORA), rowblk(ROPE_PAD),
                  pl.BlockSpec(memory_space=pl.ANY), pl.BlockSpec(memory_space=pl.ANY)],
        out_specs=pl.BlockSpec((DEC_SEQ, A_HEADS * KV_LORA), lambda b, pt: (b, 0)),
        scratch_shapes=[pltpu.VMEM((RING_SLOTS, ppc, PAGE_SIZE, KV_LORA), F32),
                        pltpu.VMEM((RING_SLOTS, ppc, A_ROPE, PAGE_SIZE), F32),
                        pltpu.SemaphoreType.DMA((2, RING_SLOTS)),
                        pltpu.VMEM((rows, KV_LORA), BF16), pltpu.VMEM((rows, A_ROPE), BF16),
                        pltpu.VMEM((KEY_GROUPS, rows, 1), F32), pltpu.VMEM((KEY_GROUPS, rows, 1), F32),
                        pltpu.VMEM((KEY_GROUPS, rows, KV_LORA), F32)],
    )
    return pl.pallas_call(
        _paged_attn_kernel,
        grid_spec=grid_spec,
        out_shape=jax.ShapeDtypeStruct((DEC_BATCH * DEC_SEQ, A_HEADS * KV_LORA), BF16),
        compiler_params=_cparams(("arbitrary",)),
        name="paged_attn",
    )(page_table, ql, qr, kl, kr, cache_lat, cache_rope_t)


def _uvproj_kernel(o_ref, w_ref, yin_ref, y_ref):
    del yin_ref
    for h in range(A_HEADS):
        y = jnp.dot(o_ref[:, h * KV_LORA:(h + 1) * KV_LORA], w_ref[h], preferred_element_type=F32)
        y_ref[:, h * A_VD:(h + 1) * A_VD] = y.astype(BF16)


def _uvproj_sample(o, wuv, y_in):
    rb0 = ROW_S0 // TM
    return pl.pallas_call(
        _uvproj_kernel,
        grid=(o.shape[0] // TM,),
        in_specs=[pl.BlockSpec((TM, A_HEADS * KV_LORA), lambda i: (i, 0)),
                  pl.BlockSpec(wuv.shape, lambda i: (0, 0, 0)),
                  pl.BlockSpec(memory_space=pl.ANY)],
        out_specs=pl.BlockSpec((TM, A_HEADS * A_VD), lambda i: (rb0 + i, 0)),
        out_shape=jax.ShapeDtypeStruct(y_in.shape, BF16),
        input_output_aliases={2: 0},
        compiler_params=_cparams(("parallel",)),
        name="mla_uvproj_sample",
    )(o, wuv, y_in)


def _merge_kernel(ym_ref, ya_ref, wm_ref, wa_ref, gm_ref, ga_ref, t_ref):
    y_m = jnp.dot(ym_ref[...], wm_ref[...], preferred_element_type=F32)
    y_a = jnp.dot(ya_ref[...], wa_ref[...], preferred_element_type=F32)
    t_ref[...] = (gm_ref[...] * y_m + ga_ref[...] * y_a).astype(BF16)


def _merge(ym, ya, wm, wa, og):
    tn = 1024
    nb = D_MODEL // tn
    return pl.pallas_call(
        _merge_kernel,
        grid=(ROWS // TM, nb),
        in_specs=[pl.BlockSpec((TM, D_MODEL), lambda i, j: (i, 0)),
                  pl.BlockSpec((TM, D_MODEL), lambda i, j: (i, 0)),
                  pl.BlockSpec((D_MODEL, tn), lambda i, j: (0, j)),
                  pl.BlockSpec((D_MODEL, tn), lambda i, j: (0, j)),
                  pl.BlockSpec((TM, tn), lambda i, j: (i, nb + j)),
                  pl.BlockSpec((TM, tn), lambda i, j: (i, 2 * nb + j))],
        out_specs=pl.BlockSpec((TM, tn), lambda i, j: (i, j)),
        out_shape=jax.ShapeDtypeStruct((ROWS, D_MODEL), BF16),
        compiler_params=_cparams(("parallel", "arbitrary")),
        name="branch_merge",
    )(ym, ya, wm, wa, og, og)


def _layer_norm(z, g, b):
    mu = jnp.mean(z, axis=-1, keepdims=True)
    d = z - mu
    var = jnp.mean(d * d, axis=-1, keepdims=True)
    return d * lax.rsqrt(var + LN_EPS) * g + b


def _outln_kernel(t_ref, w_ref, xp_ref, xr_ref, g_ref, b_ref, xf_ref, xb_ref):
    mix = jnp.dot(t_ref[...], w_ref[...], preferred_element_type=F32)
    x1 = _layer_norm(ALPHA * _x_tile(xp_ref, xr_ref) + mix, g_ref[...], b_ref[...])
    xf_ref[...] = x1
    xb_ref[...] = x1.astype(BF16)


def _outln(t, w, xp, xr, g, b):
    row = pl.BlockSpec((TM, D_MODEL), lambda i: (i, 0))
    vec = pl.BlockSpec((1, D_MODEL), lambda i: (0, 0))
    sd = jax.ShapeDtypeStruct
    return pl.pallas_call(
        _outln_kernel,
        grid=(ROWS // TM,),
        in_specs=[row, pl.BlockSpec((D_MODEL, D_MODEL), lambda i: (0, 0))] + _x_specs() + [vec, vec],
        out_specs=[row, row],
        out_shape=[sd((ROWS, D_MODEL), F32), sd((ROWS, D_MODEL), BF16)],
        compiler_params=_cparams(("parallel",)),
        name="out_proj_ln",
    )(t, w, xp, xr, g, b)


def _mlp_kernel(xb_ref, wu_ref, wd_ref, xf_ref, g_ref, b_ref, y_ref, acc_ref):
    f = pl.program_id(1)
    h = jnp.dot(xb_ref[...], wu_ref[...], preferred_element_type=F32)
    h = jnp.square(jnp.maximum(h, 0.0)).astype(BF16)
    part = jnp.dot(h, wd_ref[...], preferred_element_type=F32)

    @pl.when(f == 0)
    def _():
        acc_ref[...] = part

    @pl.when(f > 0)
    def _():
        acc_ref[...] += part

    @pl.when(f == pl.num_programs(1) - 1)
    def _():
        y_ref[...] = _layer_norm(ALPHA * xf_ref[...] + acc_ref[...], g_ref[...], b_ref[...])


def _mlp(xb, wu, wd, xf, g, b):
    tf = 1024
    row = pl.BlockSpec((TM, D_MODEL), lambda i, f: (i, 0))
    vec = pl.BlockSpec((1, D_MODEL), lambda i, f: (0, 0))
    return pl.pallas_call(
        _mlp_kernel,
        grid=(ROWS // TM, D_FF // tf),
        in_specs=[row, pl.BlockSpec((D_MODEL, tf), lambda i, f: (0, f)),
                  pl.BlockSpec((tf, D_MODEL), lambda i, f: (f, 0)), row, vec, vec],
        out_specs=row,
        out_shape=jax.ShapeDtypeStruct((ROWS, D_MODEL), F32),
        scratch_shapes=[pltpu.VMEM((TM, D_MODEL), F32)],
        compiler_params=_cparams(("parallel", "arbitrary")),
        name="mlp_ln",
    )(xb, wu, wd, xf, g, b)


def _rot_cols(w):
    half = A_ROPE // 2
    return jnp.concatenate([-w[..., half:], w[..., :half]], axis=-1)


def _pad_cols(w, n):
    return jnp.pad(w, [(0, 0)] * (w.ndim - 1) + [(0, n - w.shape[-1])])


def _rope_tables():
    half = A_ROPE // 2
    pos = jnp.concatenate([
        N_META + jnp.arange(SEQ, dtype=jnp.int32),
        jnp.tile(PAST_LEN + jnp.arange(DEC_SEQ, dtype=jnp.int32), DEC_BATCH),
        jnp.arange(N_META, dtype=jnp.int32),
        jnp.zeros((ROWS - ROW_M0 - N_META,), jnp.int32)])
    inv = ROPE_BASE ** (-jnp.arange(half, dtype=F32) / half)
    ang = pos.astype(F32)[:, None] * inv
    cos, sin = jnp.cos(ang), jnp.sin(ang)
    return (_pad_cols(jnp.concatenate([cos, cos], axis=-1), ROPE_PAD),
            _pad_cols(jnp.concatenate([sin, sin], axis=-1), ROPE_PAD))


def kernel(x_prompt, x_sample, cache_latent, cache_krope, state_C, state_n, state_m, page_table, meta, w_in, b_in, mh_g, q_norm_g, kv_norm_g, w_uq, w_uk, w_uv, w_br_m, w_br_a, w_out, ln1_g, ln1_b, w_up, w_down, ln2_g, ln2_b):
    wi, bi = w_in[0], b_in[0]
    x_p = x_prompt[0]
    x_r = jnp.concatenate([
        x_sample.reshape(DEC_BATCH * DEC_SEQ, D_MODEL), meta,
        jnp.zeros((ROWS - ROW_M0 - N_META, D_MODEL), F32)], axis=0)
    cos_t, sin_t = _rope_tables()

    ones = lambda n: jnp.ones((1, n), F32)
    k_scale = jnp.concatenate([jnp.ones((1, 1024), F32), jnp.full((1, 1024), M_DK ** -0.5, F32),
                               jnp.ones((1, 2048), F32)], axis=1)
    qkv = _matmul(x_p, x_r, wi[:, OFF_Q:OFF_O].astype(BF16), bi[None, OFF_Q:OFF_O], k_scale,
                  act=None, out_dtype=BF16, tn=2048, name="inproj_qkv")
    w_og = jnp.concatenate([wi[:, OFF_O:OFF_I], wi[:, OFF_G:]], axis=1).astype(BF16)
    b_og = jnp.concatenate([bi[OFF_O:OFF_I], bi[OFF_G:]])[None]
    og = _matmul(x_p, x_r, w_og, b_og, ones(w_og.shape[1]), act="sigmoid", out_dtype=F32, tn=2048,
                 name="inproj_gates")

    def tail_cols(a):
        kr = a[..., OFF_KR:OFF_G]
        return jnp.concatenate([
            a[..., OFF_AQ:OFF_KR], _pad_cols(kr, ROPE_PAD), _pad_cols(_rot_cols(kr), ROPE_PAD),
            _pad_cols(a[..., OFF_I:OFF_AQ], LANE)], axis=-1)

    qn, ckv_f, ckv_b, kr_f, kr_b, gates = _tail(
        x_p, x_r, tail_cols(wi).astype(BF16), tail_cols(bi)[None], q_norm_g, kv_norm_g, cos_t, sin_t)

    mhg = mh_g
    ym = jnp.zeros((ROWS, M_HEADS * M_DV), BF16)
    zc = jnp.zeros((1, M_HEADS, M_DK, M_DV), F32)
    zn = jnp.zeros((1, M_HEADS, M_DK), F32)
    zm = jnp.zeros((1, M_HEADS, 1), F32)
    ym, c_m, n_m, m_m = _mlstm(qkv, gates, og, mhg, zc, zn, zm, ym, length=N_META, nseq=1, nsteps=1,
                               row0=ROW_M0, carry=True, name="mlstm_meta")
    ym, p_c, p_n, p_m = _mlstm(qkv, gates, og, mhg, c_m, n_m, m_m, ym, length=M_CHUNK_P, nseq=1,
                               nsteps=SEQ // M_CHUNK_P, row0=0, carry=True, name="mlstm_prompt")
    ym, s_c, s_n, s_m = _mlstm(qkv, gates, og, mhg, state_C[0], state_n[0], state_m[0][..., None], ym,
                               length=DEC_SEQ, nseq=SEQ_PER_STEP, nsteps=DEC_BATCH // SEQ_PER_STEP,
                               row0=ROW_S0, carry=False, name="mlstm_sample")

    wq = w_uq[0].reshape(Q_LORA, A_HEADS, A_NOPE + A_ROPE).transpose(1, 0, 2)
    wq_r = wq[..., A_NOPE:]
    wq_p = jnp.concatenate([wq[..., :A_NOPE], _pad_cols(wq_r, ROPE_PAD), _pad_cols(_rot_cols(wq_r), ROPE_PAD)],
                           axis=-1).astype(BF16)
    wuk = w_uk[0].transpose(1, 2, 0).astype(BF16)
    wuv = w_uv[0].transpose(1, 0, 2).astype(BF16)
    q_lat, q_rope, q_heads = _qproj(qn, wq_p, wuk, cos_t, sin_t)
    wk = w_uk[0].transpose(1, 0, 2).astype(BF16)
    wvt = w_uv[0].transpose(1, 2, 0).astype(BF16)
    k_heads, v_heads_t = _kvproj(ckv_b, kr_b, wk, wvt)
    ya = jnp.zeros((ROWS, A_HEADS * A_VD), BF16)
    ya = _prompt_attn(q_heads, k_heads, v_heads_t, v_heads_t[:, ROW_M0:ROW_M0 + N_META], ya)
    o_s = _paged_attn(page_table, q_lat, q_rope, ckv_b, kr_b, cache_latent[0],
                      jnp.swapaxes(cache_krope[0], 1, 2))
    ya = _uvproj_sample(o_s, wuv, ya)

    t = _merge(ym, ya, w_br_m[0].astype(BF16), w_br_a[0].astype(BF16), og)
    x1f, x1b = _outln(t, w_out[0].astype(BF16), x_p, x_r, ln1_g, ln1_b)
    y = _mlp(x1b, w_up[0].astype(BF16), w_down[0].astype(BF16), x1f, ln2_g, ln2_b)

    def prompt_rows(a, width):
        return jnp.concatenate([a[ROW_M0:ROW_M0 + N_META, :width], a[:SEQ, :width]], axis=0)[None, None]

    def sample_rows(a, width):
        return a[ROW_S0:ROW_M0, :width].reshape(1, DEC_BATCH, DEC_SEQ, width)

    return (y[:SEQ][None], y[ROW_S0:ROW_M0].reshape(DEC_BATCH, DEC_SEQ, D_MODEL),
            prompt_rows(ckv_f, KV_LORA), prompt_rows(kr_f, A_ROPE),
            p_c[None], p_n[None], p_m[..., 0][None],
            sample_rows(ckv_f, KV_LORA), sample_rows(kr_f, A_ROPE),
            s_c[None], s_n[None], s_m[..., 0][None])
```

```python
import functools
import math

import numpy as np
import jax
import jax.numpy as jnp
from jax import lax
from jax.experimental import pallas as pl
from jax.experimental.pallas import tpu as pltpu

F32 = jnp.float32
BF16 = jnp.bfloat16

D_MODEL = 2048
SEQ = 8192
DEC_BATCH = 128
DEC_SEQ = 8
PAST_LEN = 16384
PAGE_SIZE = 128
N_META = 16
M_HEADS = 8
M_DK = 128
M_DV = 256
A_HEADS = 16
A_NOPE = 128
A_ROPE = 64
A_VD = 128
Q_LORA = 512
KV_LORA = 256
ROPE_BASE = 10000.0
ATTN_SCALE = (A_NOPE + A_ROPE) ** -0.5
D_FF = 4 * D_MODEL
ALPHA = 2.0 ** 0.25
LN_EPS = 1e-5
RMS_EPS = 1e-6

OFF_Q, OFF_K, OFF_V, OFF_O, OFF_I, OFF_F, OFF_AQ, OFF_CKV, OFF_KR, OFF_G = (
    0, 1024, 2048, 4096, 6144, 6152, 6160, 6672, 6928, 6992)
D_IN = 11088

ROW_S0 = SEQ
ROW_M0 = SEQ + DEC_BATCH * DEC_SEQ
ROWS = 9728
TM = 512
LANE = 128
ROPE_PAD = LANE

M_CHUNK_P = 256
SEQ_PER_STEP = 2
TQ = 256
TK = 512
PAGES_PER_STEP = 32
RING_SLOTS = 4
KEY_GROUPS = 2
PAGED_LOOKAHEAD = 1
VMEM_LIMIT = 56 * 1024 * 1024


def _cparams(sem):
    return pltpu.CompilerParams(dimension_semantics=sem, vmem_limit_bytes=VMEM_LIMIT)


PROMPT_TILES = SEQ // TM


def _x_specs():
    return [pl.BlockSpec((TM, D_MODEL), lambda i, *_: (jnp.minimum(i, PROMPT_TILES - 1), 0)),
            pl.BlockSpec((TM, D_MODEL), lambda i, *_: (jnp.maximum(i - PROMPT_TILES, 0), 0))]


def _x_tile(xp_ref, xr_ref):
    return jnp.where(pl.program_id(0) < PROMPT_TILES, xp_ref[...], xr_ref[...])


def _mm_kernel(xp_ref, xr_ref, w_ref, b_ref, s_ref, o_ref, xb_ref, *, act):
    is_prompt = pl.program_id(0) < PROMPT_TILES

    @pl.when(pl.program_id(1) == 0)
    def _():
        xb_ref[...] = jnp.where(is_prompt, xp_ref[...], xr_ref[...]).astype(BF16)

    y = jnp.dot(xb_ref[...], w_ref[...], preferred_element_type=F32)
    y = (y + b_ref[...]) * s_ref[...]
    if act == "sigmoid":
        y = jax.nn.sigmoid(y)
    o_ref[...] = y.astype(o_ref.dtype)


def _matmul(xp, xr, w, b, s, *, act, out_dtype, tn, name):
    k, n = w.shape
    return pl.pallas_call(
        functools.partial(_mm_kernel, act=act),
        grid=(ROWS // TM, n // tn),
        in_specs=_x_specs() + [
            pl.BlockSpec((k, tn), lambda i, j: (0, j)),
            pl.BlockSpec((1, tn), lambda i, j: (0, j)),
            pl.BlockSpec((1, tn), lambda i, j: (0, j)),
        ],
        out_specs=pl.BlockSpec((TM, tn), lambda i, j: (i, j)),
        out_shape=jax.ShapeDtypeStruct((ROWS, n), out_dtype),
        scratch_shapes=[pltpu.VMEM((TM, k), BF16)],
        compiler_params=_cparams(("parallel", "arbitrary")),
        name=name,
    )(xp, xr, w, b, s)


T_AQ, T_CKV, T_KR, T_KROT, T_G, T_END = 0, 512, 768, 896, 1024, 1152


def _tail_kernel(xp_ref, xr_ref, w_ref, b_ref, qg_ref, kg_ref, cos_ref, sin_ref,
                 qn_ref, cf_ref, cb_ref, rf_ref, rb_ref, g_ref):
    xb = _x_tile(xp_ref, xr_ref).astype(BF16)
    acc = jnp.dot(xb, w_ref[...], preferred_element_type=F32) + b_ref[...]
    aq = acc[:, T_AQ:T_CKV]
    qn = aq * lax.rsqrt(jnp.mean(aq * aq, axis=-1, keepdims=True) + RMS_EPS) * qg_ref[...]
    qn_ref[...] = qn.astype(BF16)
    ckv = acc[:, T_CKV:T_KR]
    c = ckv * lax.rsqrt(jnp.mean(ckv * ckv, axis=-1, keepdims=True) + RMS_EPS) * kg_ref[...]
    cf_ref[...] = c
    cb_ref[...] = c.astype(BF16)
    kr = acc[:, T_KR:T_KROT] * cos_ref[...] + acc[:, T_KROT:T_G] * sin_ref[...]
    rf_ref[...] = kr
    rb_ref[...] = kr.astype(BF16)
    g_ref[...] = acc[:, T_G:T_END]


def _tail(xp, xr, w, b, qg, kg, cos, sin):
    row = lambda n: pl.BlockSpec((TM, n), lambda i: (i, 0))
    full = lambda a: pl.BlockSpec(a.shape, lambda i: (0,) * a.ndim)
    sd = jax.ShapeDtypeStruct
    return pl.pallas_call(
        _tail_kernel,
        grid=(ROWS // TM,),
        in_specs=_x_specs() + [full(w), full(b), full(qg), full(kg), row(ROPE_PAD), row(ROPE_PAD)],
        out_specs=[row(Q_LORA), row(KV_LORA), row(KV_LORA), row(ROPE_PAD), row(ROPE_PAD), row(LANE)],
        out_shape=[sd((ROWS, Q_LORA), BF16), sd((ROWS, KV_LORA), F32), sd((ROWS, KV_LORA), BF16),
                   sd((ROWS, ROPE_PAD), F32), sd((ROWS, ROPE_PAD), BF16), sd((ROWS, LANE), F32)],
        compiler_params=_cparams(("parallel",)),
        name="inproj_tail",
    )(xp, xr, w, b, qg, kg, cos, sin)


def _log_sigmoid(x):
    return jnp.minimum(x, 0.0) - jnp.log1p(jnp.exp(-jnp.abs(x)))


def _mlstm_head(q, k, v, ig_c, ig_r, lf_c, lf_r, c0, n0, m0, length):
    ti = lax.broadcasted_iota(jnp.int32, (length, length), 0)
    si = lax.broadcasted_iota(jnp.int32, (length, length), 1)
    causal = si <= ti
    b_c = jnp.sum(jnp.where(causal, lf_r, 0.0), axis=1, keepdims=True)
    b_r = jnp.sum(jnp.where(ti <= si, lf_c, 0.0), axis=0, keepdims=True)
    dlog = jnp.where(causal, b_c - b_r + ig_r, -jnp.inf)
    m = jnp.maximum(b_c + m0, jnp.max(dlog, axis=1, keepdims=True))
    w = jnp.exp(dlog - m)
    inter = jnp.exp(b_c + m0 - m)
    s = lax.dot_general(q, k, (((1,), (1,)), ((), ())), preferred_element_type=F32) * w
    qc = jnp.dot(q, c0.astype(BF16), preferred_element_type=F32)
    num = jnp.dot(s.astype(BF16), v, preferred_element_type=F32) + inter * qc
    qn = jnp.sum(q.astype(F32) * n0, axis=1, keepdims=True)
    nq = jnp.sum(s, axis=1, keepdims=True) + inter * qn
    h = num / jnp.maximum(jnp.abs(nq), jnp.exp(-m))
    m_end = m[length - 1:length, :]
    b_end = b_c[length - 1:length, :]
    w_end = jnp.exp(b_end - b_c + ig_c - m_end)
    dec = jnp.exp(b_end + m0 - m_end)
    kw = k.astype(F32) * w_end
    c = dec * c0 + lax.dot_general(kw.astype(BF16), v, (((0,), (0,)), ((), ())),
                                   preferred_element_type=F32)
    n = dec * n0 + jnp.sum(kw, axis=0, keepdims=True)
    return h, c, n, m_end


def _mlstm_kernel(q_ref, k_ref, v_ref, g_ref, og_ref, mhg_ref, c0_ref, n0_ref, m0_ref, yin_ref,
                  y_ref, c_ref, n_ref, m_ref, cs_ref, ns_ref, ms_ref, *, length, nseq, carry):
    del yin_ref
    step = pl.program_id(0)
    if carry:
        @pl.when(step == 0)
        def _():
            cs_ref[...] = c0_ref[...]
            ns_ref[...] = n0_ref[...]
            ms_ref[...] = m0_ref[...]
        c_src, n_src, m_src = cs_ref, ns_ref, ms_ref
    else:
        c_src, n_src, m_src = c0_ref, n0_ref, m0_ref

    def head_stages(b, h, g, gt, lfc_all, lfr_all):
        r0 = b * length
        q = q_ref[r0:r0 + length, h * M_DK:(h + 1) * M_DK]
        k = k_ref[r0:r0 + length, h * M_DK:(h + 1) * M_DK]
        v = v_ref[r0:r0 + length, h * M_DV:(h + 1) * M_DV]
        return _mlstm_head(
            q, k, v,
            g[:, h:h + 1], gt[h:h + 1, :],
            lfc_all[:, M_HEADS + h:M_HEADS + h + 1], lfr_all[M_HEADS + h:M_HEADS + h + 1, :],
            c_src[b, h], n_src[b, h:h + 1, :], m_src[b, h:h + 1, :], length)

    def readout(b, h, result):
        hh, c, n, m_end = result
        r0 = b * length
        mu = jnp.mean(hh, axis=1, keepdims=True)
        d = hh - mu
        var = jnp.mean(d * d, axis=1, keepdims=True)
        hn = d * lax.rsqrt(var + LN_EPS)
        cols = slice(h * M_DV, (h + 1) * M_DV)
        y_ref[r0:r0 + length, cols] = (hn * mhg_ref[:, cols] * og_ref[r0:r0 + length, cols]).astype(BF16)
        c_dst, n_dst, m_dst = (cs_ref, ns_ref, ms_ref) if carry else (c_ref, n_ref, m_ref)
        c_dst[b, h] = c
        n_dst[b, h:h + 1, :] = n
        m_dst[b, h:h + 1, :] = m_end

    for b in range(nseq):
        r0 = b * length
        g = g_ref[r0:r0 + length, :]
        gt = g.T
        lfc_all = _log_sigmoid(g)
        lfr_all = _log_sigmoid(gt)
        for h in range(M_HEADS):
            readout(b, h, head_stages(b, h, g, gt, lfc_all, lfr_all))

    if carry:
        @pl.when(step == pl.num_programs(0) - 1)
        def _():
            c_ref[...] = cs_ref[...]
            n_ref[...] = ns_ref[...]
            m_ref[...] = ms_ref[...]


def _mlstm(qkv, gates, og, mhg, c0, n0, m0, y_in, *, length, nseq, nsteps, row0, carry, name):
    rows = nseq * length
    rb0 = row0 // rows
    nstate = c0.shape[0]
    if carry:
        st = lambda *tail: pl.BlockSpec((nstate,) + tail, lambda s: (0,) * (1 + len(tail)))
    else:
        st = lambda *tail: pl.BlockSpec((nseq,) + tail, lambda s: (s,) + (0,) * len(tail))
    sd = jax.ShapeDtypeStruct
    return pl.pallas_call(
        functools.partial(_mlstm_kernel, length=length, nseq=nseq, carry=carry),
        grid=(nsteps,),
        in_specs=[
            pl.BlockSpec((rows, M_HEADS * M_DK), lambda s: (rb0 + s, 0)),
            pl.BlockSpec((rows, M_HEADS * M_DK), lambda s: (rb0 + s, 1)),
            pl.BlockSpec((rows, M_HEADS * M_DV), lambda s: (rb0 + s, 1)),
            pl.BlockSpec((rows, LANE), lambda s: (rb0 + s, 0)),
            pl.BlockSpec((rows, M_HEADS * M_DV), lambda s: (rb0 + s, 0)),
            pl.BlockSpec((1, M_HEADS * M_DV), lambda s: (0, 0)),
            st(M_HEADS, M_DK, M_DV), st(M_HEADS, M_DK), st(M_HEADS, 1),
            pl.BlockSpec(memory_space=pl.ANY),
        ],
        out_specs=[
            pl.BlockSpec((rows, M_HEADS * M_DV), lambda s: (rb0 + s, 0)),
            st(M_HEADS, M_DK, M_DV), st(M_HEADS, M_DK), st(M_HEADS, 1),
        ],
        out_shape=[sd(y_in.shape, BF16), sd(c0.shape, F32), sd(n0.shape, F32), sd(m0.shape, F32)],
        scratch_shapes=[pltpu.VMEM((nstate, M_HEADS, M_DK, M_DV), F32),
                        pltpu.VMEM((nstate, M_HEADS, M_DK), F32),
                        pltpu.VMEM((nstate, M_HEADS, 1), F32)] if carry else
                       [pltpu.VMEM((8, LANE), F32)] * 3,
        input_output_aliases={9: 0},
        compiler_params=_cparams(("arbitrary",)),
        name=name,
    )(qkv, qkv, qkv, gates, og, mhg, c0, n0, m0, y_in)


QW = A_NOPE + 2 * ROPE_PAD


HEAD_W = A_NOPE + ROPE_PAD


def _qproj_kernel(qn_ref, wq_ref, wuk_ref, cos_ref, sin_ref, ql_ref, qr_ref, qh_ref):
    qn = qn_ref[...]
    for h in range(A_HEADS):
        q = jnp.dot(qn, wq_ref[h], preferred_element_type=F32)
        q_nope = q[:, :A_NOPE].astype(BF16)
        ql = jnp.dot(q_nope, wuk_ref[h], preferred_element_type=F32)
        ql_ref[:, h * KV_LORA:(h + 1) * KV_LORA] = ql.astype(BF16)
        qr_f = q[:, A_NOPE:A_NOPE + ROPE_PAD] * cos_ref[...] + q[:, A_NOPE + ROPE_PAD:] * sin_ref[...]
        qr_ref[:, h * ROPE_PAD:(h + 1) * ROPE_PAD] = qr_f.astype(BF16)
        qh_ref[:, h * HEAD_W:h * HEAD_W + A_NOPE] = (q[:, :A_NOPE] * C_LOG2).astype(BF16)
        qh_ref[:, h * HEAD_W + A_NOPE:(h + 1) * HEAD_W] = (qr_f * C_LOG2).astype(BF16)


def _qproj(qn, wq, wuk, cos, sin):
    sd = jax.ShapeDtypeStruct
    row = lambda n: pl.BlockSpec((TM, n), lambda i: (i, 0))
    full = lambda a: pl.BlockSpec(a.shape, lambda i: (0,) * a.ndim)
    return pl.pallas_call(
        _qproj_kernel,
        grid=(ROWS // TM,),
        in_specs=[row(Q_LORA), full(wq), full(wuk), row(ROPE_PAD), row(ROPE_PAD)],
        out_specs=[row(A_HEADS * KV_LORA), row(A_HEADS * ROPE_PAD), row(A_HEADS * HEAD_W)],
        out_shape=[sd((ROWS, A_HEADS * KV_LORA), BF16), sd((ROWS, A_HEADS * ROPE_PAD), BF16),
                   sd((ROWS, A_HEADS * HEAD_W), BF16)],
        compiler_params=_cparams(("parallel",)),
        name="mla_qproj",
    )(qn, wq, wuk, cos, sin)


def _kvproj_kernel(c_ref, kr_ref, wk_ref, wvt_ref, k_ref, vt_ref):
    c = c_ref[...]
    kr = kr_ref[...]
    for h in range(A_HEADS):
        kn = jnp.dot(c, wk_ref[h], preferred_element_type=F32)
        k_ref[:, h * HEAD_W:h * HEAD_W + A_NOPE] = kn.astype(BF16)
        k_ref[:, h * HEAD_W + A_NOPE:(h + 1) * HEAD_W] = kr
        vt = lax.dot_general(wvt_ref[h], c, NT_DIMS, preferred_element_type=F32)
        vt_ref[h * A_VD:(h + 1) * A_VD, :] = vt.astype(BF16)


def _kvproj(c, kr, wk, wvt):
    sd = jax.ShapeDtypeStruct
    row = lambda n: pl.BlockSpec((TM, n), lambda i: (i, 0))
    full = lambda a: pl.BlockSpec(a.shape, lambda i: (0,) * a.ndim)
    return pl.pallas_call(
        _kvproj_kernel,
        grid=(ROWS // TM,),
        in_specs=[row(KV_LORA), row(ROPE_PAD), full(wk), full(wvt)],
        out_specs=[row(A_HEADS * HEAD_W), pl.BlockSpec((A_HEADS * A_VD, TM), lambda i: (0, i))],
        out_shape=[sd((ROWS, A_HEADS * HEAD_W), BF16), sd((A_HEADS * A_VD, ROWS), BF16)],
        compiler_params=_cparams(("parallel",)),
        name="mla_kvproj",
    )(c, kr, wk, wvt)


def _stack_heads(ref, width):
    return jnp.concatenate([ref[:, h * width:(h + 1) * width] for h in range(A_HEADS)], axis=0)


NT_DIMS = (((1,), (1,)), ((), ()))
C_LOG2 = ATTN_SCALE * math.log2(math.e)
SCORE_LOOKAHEAD = 4


def _softmax_update_t(st, vt, g, m_ref, l_ref, acc_ref, init):
    mx = jnp.max(st, axis=0, keepdims=True)
    if init:
        m_new = mx
        p = jnp.exp2(st - m_new)
        l_ref[g] = jnp.sum(p, axis=0, keepdims=True)
        acc_ref[g] = jnp.dot(vt, p.astype(BF16), preferred_element_type=F32)
    else:
        m_old = m_ref[g]
        m_new = jnp.maximum(m_old, mx)
        alpha = jnp.exp2(m_old - m_new)
        p = jnp.exp2(st - m_new)
        l_ref[g] = alpha * l_ref[g] + jnp.sum(p, axis=0, keepdims=True)
        acc_ref[g] = alpha * acc_ref[g] + jnp.dot(vt, p.astype(BF16), preferred_element_type=F32)
    m_ref[g] = m_new


def _prompt_attn_kernel(qi_ref, kj_ref, first_ref, last_ref, qpos_ref,
                        q_ref, k_ref, vt_ref, mk_ref, mvt_ref, yin_ref,
                        y_ref, m_ref, l_ref, acc_ref):
    del yin_ref
    s_idx = pl.program_id(0)
    qpos = qpos_ref[s_idx] + lax.broadcasted_iota(jnp.int32, (1, TQ), 1)

    def scores(keys_ref, h):
        cols = slice(h * HEAD_W, (h + 1) * HEAD_W)
        return lax.dot_general(keys_ref[:, cols], q_ref[:, cols], NT_DIMS,
                               preferred_element_type=F32)

    def sweep(keys_ref, v_t_ref, kpos, init):
        ahead = [scores(keys_ref, h) for h in range(SCORE_LOOKAHEAD)]
        for h in range(A_HEADS):
            st = ahead.pop(0)
            if h + SCORE_LOOKAHEAD < A_HEADS:
                ahead.append(scores(keys_ref, h + SCORE_LOOKAHEAD))
            if kpos is not None:
                st = jnp.where(kpos <= qpos, st, -jnp.inf)
            _softmax_update_t(st, v_t_ref[h * A_VD:(h + 1) * A_VD, :], h, m_ref, l_ref, acc_ref, init=init)

    @pl.when(first_ref[s_idx] == 1)
    def _():
        kpos = lax.broadcasted_iota(jnp.int32, (N_META, 1), 0)
        sweep(mk_ref, mvt_ref, kpos, init=True)

    def key_block(masked):
        kpos = N_META + kj_ref[s_idx] * TK + lax.broadcasted_iota(jnp.int32, (TK, 1), 0)
        sweep(k_ref, vt_ref, kpos if masked else None, init=False)

    pl.when(last_ref[s_idx] == 0)(functools.partial(key_block, False))
    pl.when(last_ref[s_idx] == 1)(functools.partial(key_block, True))

    @pl.when(last_ref[s_idx] == 1)
    def _():
        for h in range(A_HEADS):
            o = (acc_ref[h] / l_ref[h]).T
            y_ref[:, h * A_VD:(h + 1) * A_VD] = o.astype(BF16)


def _prompt_schedule():
    qi, kj, first, last, qpos = [], [], [], [], []
    for i in range(SEQ // TQ):
        nk = (i * TQ + TQ - 1) // TK + 1
        for j in range(nk):
            qi.append(i); kj.append(j); first.append(int(j == 0)); last.append(int(j == nk - 1))
            qpos.append(N_META + i * TQ)
    qi.append(ROW_M0 // TQ); kj.append(0); first.append(1); last.append(1); qpos.append(0)
    return [jnp.asarray(np.asarray(a, np.int32)) for a in (qi, kj, first, last, qpos)]


def _prompt_attn(qh, kh, vt, mvt, y_in):
    sched = _prompt_schedule()
    nsteps = int(sched[0].shape[0])
    mblk = ROW_M0 // N_META
    grid_spec = pltpu.PrefetchScalarGridSpec(
        num_scalar_prefetch=5,
        grid=(nsteps,),
        in_specs=[
            pl.BlockSpec((TQ, A_HEADS * HEAD_W), lambda s, qi, kj, f, l, p: (qi[s], 0)),
            pl.BlockSpec((TK, A_HEADS * HEAD_W), lambda s, qi, kj, f, l, p: (kj[s], 0)),
            pl.BlockSpec((A_HEADS * A_VD, TK), lambda s, qi, kj, f, l, p: (0, kj[s])),
            pl.BlockSpec((N_META, A_HEADS * HEAD_W), lambda s, qi, kj, f, l, p: (mblk, 0)),
            pl.BlockSpec(mvt.shape, lambda s, qi, kj, f, l, p: (0, 0)),
            pl.BlockSpec(memory_space=pl.ANY),
        ],
        out_specs=pl.BlockSpec((TQ, A_HEADS * A_VD), lambda s, qi, kj, f, l, p: (qi[s], 0)),
        scratch_shapes=[pltpu.VMEM((A_HEADS, 1, TQ), F32), pltpu.VMEM((A_HEADS, 1, TQ), F32),
                        pltpu.VMEM((A_HEADS, A_VD, TQ), F32)],
    )
    return pl.pallas_call(
        _prompt_attn_kernel,
        grid_spec=grid_spec,
        out_shape=jax.ShapeDtypeStruct(y_in.shape, BF16),
        input_output_aliases={5 + 5: 0},
        compiler_params=_cparams(("arbitrary",)),
        name="prompt_attn",
    )(*sched, qh, kh, vt, kh, mvt, y_in)


def _paged_attn_kernel(pt_ref, ql_ref, qr_ref, nl_ref, nr_ref, lat_hbm, rope_hbm, o_ref,
                       lat_buf, rope_buf, sem, qls_ref, qrs_ref, m_ref, l_ref, acc_ref):
    b = pl.program_id(0)
    nseq = pl.num_programs(0)
    ppc = PAGES_PER_STEP
    nch = (PAST_LEN // PAGE_SIZE) // ppc
    ahead = RING_SLOTS - 1

    def page_copies(seq, chunk, slot):
        cps = []
        for p in range(ppc):
            page = pt_ref[seq, chunk * ppc + p]
            cps.append(pltpu.make_async_copy(lat_hbm.at[page], lat_buf.at[slot, p], sem.at[0, slot]))
            cps.append(pltpu.make_async_copy(rope_hbm.at[page], rope_buf.at[slot, p], sem.at[1, slot]))
        return cps

    def start(seq, chunk, slot):
        for cp in page_copies(seq, chunk, slot):
            cp.start()

    @pl.when(b == 0)
    def _():
        for c in range(ahead):
            start(0, c, c)

    qls_ref[...] = _stack_heads(ql_ref, KV_LORA)
    qrs_ref[...] = _stack_heads(qr_ref, ROPE_PAD)[:, :A_ROPE]
    m_ref[...] = jnp.full(m_ref.shape, -jnp.inf, F32)
    l_ref[...] = jnp.zeros(l_ref.shape, F32)
    acc_ref[...] = jnp.zeros(acc_ref.shape, F32)

    def update(g, s, vals):
        m_old = m_ref[g]
        m_new = jnp.maximum(m_old, jnp.max(s, axis=1, keepdims=True))
        alpha = jnp.exp2(m_old - m_new)
        p = jnp.exp2(s - m_new)
        m_ref[g] = m_new
        l_ref[g] = alpha * l_ref[g] + jnp.sum(p, axis=1, keepdims=True)
        acc_ref[g] = alpha * acc_ref[g] + jnp.dot(p.astype(BF16), vals, preferred_element_type=F32)

    def chunk_scores(chunk):
        slot = chunk % RING_SLOTS
        for cp in page_copies(b, chunk, slot):
            cp.wait()
        kl = jnp.concatenate([lat_buf[slot, p] for p in range(ppc)], axis=0).astype(BF16)
        krt = jnp.concatenate([rope_buf[slot, p] for p in range(ppc)], axis=1).astype(BF16)
        s = (lax.dot_general(qls_ref[...], kl, NT_DIMS, preferred_element_type=F32)
             + jnp.dot(qrs_ref[...], krt, preferred_element_type=F32)) * C_LOG2
        return s, kl

    pending = [chunk_scores(c) for c in range(PAGED_LOOKAHEAD)]
    for c in range(nch):
        n = c + ahead
        if n < nch:
            start(b, n, n % RING_SLOTS)
        else:
            pl.when(b + 1 < nseq)(functools.partial(start, b + 1, n - nch, n % RING_SLOTS))
        if c + PAGED_LOOKAHEAD < nch:
            pending.append(chunk_scores(c + PAGED_LOOKAHEAD))
        update(c % KEY_GROUPS, *pending.pop(0))

    def finalize():
        nl = nl_ref[...]
        s2 = (lax.dot_general(qls_ref[...], nl, NT_DIMS, preferred_element_type=F32)
              + lax.dot_general(qrs_ref[...], nr_ref[:, :A_ROPE], NT_DIMS,
                                preferred_element_type=F32)) * C_LOG2
        rows = A_HEADS * DEC_SEQ
        tq = lax.broadcasted_iota(jnp.int32, (rows, 1), 0) & (DEC_SEQ - 1)
        tk = lax.broadcasted_iota(jnp.int32, (1, DEC_SEQ), 1)
        update(0, jnp.where(tk <= tq, s2, -jnp.inf), nl)
        m_tot = m_ref[0]
        for g in range(1, KEY_GROUPS):
            m_tot = jnp.maximum(m_tot, m_ref[g])
        l_tot = jnp.zeros((rows, 1), F32)
        acc = jnp.zeros((rows, KV_LORA), F32)
        for g in range(KEY_GROUPS):
            w = jnp.exp2(m_ref[g] - m_tot)
            l_tot = l_tot + w * l_ref[g]
            acc = acc + w * acc_ref[g]
        o = acc / l_tot
        for h in range(A_HEADS):
            o_ref[:, h * KV_LORA:(h + 1) * KV_LORA] = o[h * DEC_SEQ:(h + 1) * DEC_SEQ, :].astype(BF16)

    finalize()


def _paged_attn(page_table, ql, qr, kl, kr, cache_lat, cache_rope_t):
    ppc = PAGES_PER_STEP
    rb0 = ROW_S0 // DEC_SEQ
    rows = A_HEADS * DEC_SEQ
    assert (PAST_LEN // PAGE_SIZE) % (RING_SLOTS * ppc) == 0
    rowblk = lambda width: pl.BlockSpec((DEC_SEQ, width), lambda b, pt: (rb0 + b, 0))
    grid_spec = pltpu.PrefetchScalarGridSpec(
        num_scalar_prefetch=1,
        grid=(DEC_BATCH,),
        in_specs=[rowblk(A_HEADS * KV_LORA), rowblk(A_HEADS * ROPE_PAD), rowblk(KV_LORA), rowblk(ROPE_PAD),
                  pl.BlockSpec(memory_space=pl.ANY), pl.BlockSpec(memory_space=pl.ANY)],
        out_specs=pl.BlockSpec((DEC_SEQ, A_HEADS * KV_LORA), lambda b, pt: (b, 0)),
        scratch_shapes=[pltpu.VMEM((RING_SLOTS, ppc, PAGE_SIZE, KV_LORA), F32),
                        pltpu.VMEM((RING_SLOTS, ppc, A_ROPE, PAGE_SIZE), F32),
                        pltpu.SemaphoreType.DMA((2, RING_SLOTS)),
                        pltpu.VMEM((rows, KV_LORA), BF16), pltpu.VMEM((rows, A_ROPE), BF16),
                        pltpu.VMEM((KEY_GROUPS, rows, 1), F32), pltpu.VMEM((KEY_GROUPS, rows, 1), F32),
                        pltpu.VMEM((KEY_GROUPS, rows, KV_LORA), F32)],
    )
    return pl.pallas_call(
        _paged_attn_kernel,
        grid_spec=grid_spec,
        out_shape=jax.ShapeDtypeStruct((DEC_BATCH * DEC_SEQ, A_HEADS * KV_LORA), BF16),
        compiler_params=_cparams(("arbitrary",)),
        name="paged_attn",
    )(page_table, ql, qr, kl, kr, cache_lat, cache_rope_t)


def _uvproj_kernel(o_ref, w_ref, yin_ref, y_ref):
    del yin_ref
    for h in range(A_HEADS):
        y = jnp.dot(o_ref[:, h * KV_LORA:(h + 1) * KV_LORA], w_ref[h], preferred_element_type=F32)
        y_ref[:, h * A_VD:(h + 1) * A_VD] = y.astype(BF16)


def _uvproj_sample(o, wuv, y_in):
    rb0 = ROW_S0 // TM
    return pl.pallas_call(
        _uvproj_kernel,
        grid=(o.shape[0] // TM,),
        in_specs=[pl.BlockSpec((TM, A_HEADS * KV_LORA), lambda i: (i, 0)),
                  pl.BlockSpec(wuv.shape, lambda i: (0, 0, 0)),
                  pl.BlockSpec(memory_space=pl.ANY)],
        out_specs=pl.BlockSpec((TM, A_HEADS * A_VD), lambda i: (rb0 + i, 0)),
        out_shape=jax.ShapeDtypeStruct(y_in.shape, BF16),
        input_output_aliases={2: 0},
        compiler_params=_cparams(("parallel",)),
        name="mla_uvproj_sample",
    )(o, wuv, y_in)


def _merge_kernel(ym_ref, ya_ref, wm_ref, wa_ref, gm_ref, ga_ref, t_ref):
    y_m = jnp.dot(ym_ref[...], wm_ref[...], preferred_element_type=F32)
    y_a = jnp.dot(ya_ref[...], wa_ref[...], preferred_element_type=F32)
    t_ref[...] = (gm_ref[...] * y_m + ga_ref[...] * y_a).astype(BF16)


def _merge(ym, ya, wm, wa, og):
    tn = 1024
    nb = D_MODEL // tn
    return pl.pallas_call(
        _merge_kernel,
        grid=(ROWS // TM, nb),
        in_specs=[pl.BlockSpec((TM, D_MODEL), lambda i, j: (i, 0)),
                  pl.BlockSpec((TM, D_MODEL), lambda i, j: (i, 0)),
                  pl.BlockSpec((D_MODEL, tn), lambda i, j: (0, j)),
                  pl.BlockSpec((D_MODEL, tn), lambda i, j: (0, j)),
                  pl.BlockSpec((TM, tn), lambda i, j: (i, nb + j)),
                  pl.BlockSpec((TM, tn), lambda i, j: (i, 2 * nb + j))],
        out_specs=pl.BlockSpec((TM, tn), lambda i, j: (i, j)),
        out_shape=jax.ShapeDtypeStruct((ROWS, D_MODEL), BF16),
        compiler_params=_cparams(("parallel", "arbitrary")),
        name="branch_merge",
    )(ym, ya, wm, wa, og, og)


def _layer_norm(z, g, b):
    mu = jnp.mean(z, axis=-1, keepdims=True)
    d = z - mu
    var = jnp.mean(d * d, axis=-1, keepdims=True)
    return d * lax.rsqrt(var + LN_EPS) * g + b


def _outln_kernel(t_ref, w_ref, xp_ref, xr_ref, g_ref, b_ref, xf_ref, xb_ref):
    mix = jnp.dot(t_ref[...], w_ref[...], preferred_element_type=F32)
    x1 = _layer_norm(ALPHA * _x_tile(xp_ref, xr_ref) + mix, g_ref[...], b_ref[...])
    xf_ref[...] = x1
    xb_ref[...] = x1.astype(BF16)


def _outln(t, w, xp, xr, g, b):
    row = pl.BlockSpec((TM, D_MODEL), lambda i: (i, 0))
    vec = pl.BlockSpec((1, D_MODEL), lambda i: (0, 0))
    sd = jax.ShapeDtypeStruct
    return pl.pallas_call(
        _outln_kernel,
        grid=(ROWS // TM,),
        in_specs=[row, pl.BlockSpec((D_MODEL, D_MODEL), lambda i: (0, 0))] + _x_specs() + [vec, vec],
        out_specs=[row, row],
        out_shape=[sd((ROWS, D_MODEL), F32), sd((ROWS, D_MODEL), BF16)],
        compiler_params=_cparams(("parallel",)),
        name="out_proj_ln",
    )(t, w, xp, xr, g, b)


def _mlp_kernel(xb_ref, wu_ref, wd_ref, xf_ref, g_ref, b_ref, y_ref, acc_ref):
    f = pl.program_id(1)
    h = jnp.dot(xb_ref[...], wu_ref[...], preferred_element_type=F32)
    h = jnp.square(jnp.maximum(h, 0.0)).astype(BF16)
    part = jnp.dot(h, wd_ref[...], preferred_element_type=F32)

    @pl.when(f == 0)
    def _():
        acc_ref[...] = part

    @pl.when(f > 0)
    def _():
        acc_ref[...] += part

    @pl.when(f == pl.num_programs(1) - 1)
    def _():
        y_ref[...] = _layer_norm(ALPHA * xf_ref[...] + acc_ref[...], g_ref[...], b_ref[...])


def _mlp(xb, wu, wd, xf, g, b):
    tf = 1024
    row = pl.BlockSpec((TM, D_MODEL), lambda i, f: (i, 0))
    vec = pl.BlockSpec((1, D_MODEL), lambda i, f: (0, 0))
    return pl.pallas_call(
        _mlp_kernel,
        grid=(ROWS // TM, D_FF // tf),
        in_specs=[row, pl.BlockSpec((D_MODEL, tf), lambda i, f: (0, f)),
                  pl.BlockSpec((tf, D_MODEL), lambda i, f: (f, 0)), row, vec, vec],
        out_specs=row,
        out_shape=jax.ShapeDtypeStruct((ROWS, D_MODEL), F32),
        scratch_shapes=[pltpu.VMEM((TM, D_MODEL), F32)],
        compiler_params=_cparams(("parallel", "arbitrary")),
        name="mlp_ln",
    )(xb, wu, wd, xf, g, b)


def _rot_cols(w):
    half = A_ROPE // 2
    return jnp.concatenate([-w[..., half:], w[..., :half]], axis=-1)


def _pad_cols(w, n):
    return jnp.pad(w, [(0, 0)] * (w.ndim - 1) + [(0, n - w.shape[-1])])


def _rope_tables():
    half = A_ROPE // 2
    pos = jnp.concatenate([
        N_META + jnp.arange(SEQ, dtype=jnp.int32),
        jnp.tile(PAST_LEN + jnp.arange(DEC_SEQ, dtype=jnp.int32), DEC_BATCH),
        jnp.arange(N_META, dtype=jnp.int32),
        jnp.zeros((ROWS - ROW_M0 - N_META,), jnp.int32)])
    inv = ROPE_BASE ** (-jnp.arange(half, dtype=F32) / half)
    ang = pos.astype(F32)[:, None] * inv
    cos, sin = jnp.cos(ang), jnp.sin(ang)
    return (_pad_cols(jnp.concatenate([cos, cos], axis=-1), ROPE_PAD),
            _pad_cols(jnp.concatenate([sin, sin], axis=-1), ROPE_PAD))


def kernel(x_prompt, x_sample, cache_latent, cache_krope, state_C, state_n, state_m, page_table, meta, w_in, b_in, mh_g, q_norm_g, kv_norm_g, w_uq, w_uk, w_uv, w_br_m, w_br_a, w_out, ln1_g, ln1_b, w_up, w_down, ln2_g, ln2_b):
    wi, bi = w_in[0], b_in[0]
    x_p = x_prompt[0]
    x_r = jnp.concatenate([
        x_sample.reshape(DEC_BATCH * DEC_SEQ, D_MODEL), meta,
        jnp.zeros((ROWS - ROW_M0 - N_META, D_MODEL), F32)], axis=0)
    cos_t, sin_t = _rope_tables()

    ones = lambda n: jnp.ones((1, n), F32)
    k_scale = jnp.concatenate([jnp.ones((1, 1024), F32), jnp.full((1, 1024), M_DK ** -0.5, F32),
                               jnp.ones((1, 2048), F32)], axis=1)
    qkv = _matmul(x_p, x_r, wi[:, OFF_Q:OFF_O].astype(BF16), bi[None, OFF_Q:OFF_O], k_scale,
                  act=None, out_dtype=BF16, tn=2048, name="inproj_qkv")
    w_og = jnp.concatenate([wi[:, OFF_O:OFF_I], wi[:, OFF_G:]], axis=1).astype(BF16)
    b_og = jnp.concatenate([bi[OFF_O:OFF_I], bi[OFF_G:]])[None]
    og = _matmul(x_p, x_r, w_og, b_og, ones(w_og.shape[1]), act="sigmoid", out_dtype=F32, tn=2048,
                 name="inproj_gates")

    def tail_cols(a):
        kr = a[..., OFF_KR:OFF_G]
        return jnp.concatenate([
            a[..., OFF_AQ:OFF_KR], _pad_cols(kr, ROPE_PAD), _pad_cols(_rot_cols(kr), ROPE_PAD),
            _pad_cols(a[..., OFF_I:OFF_AQ], LANE)], axis=-1)

    qn, ckv_f, ckv_b, kr_f, kr_b, gates = _tail(
        x_p, x_r, tail_cols(wi).astype(BF16), tail_cols(bi)[None], q_norm_g, kv_norm_g, cos_t, sin_t)

    mhg = mh_g
    ym = jnp.zeros((ROWS, M_HEADS * M_DV), BF16)
    zc = jnp.zeros((1, M_HEADS, M_DK, M_DV), F32)
    zn = jnp.zeros((1, M_HEADS, M_DK), F32)
    zm = jnp.zeros((1, M_HEADS, 1), F32)
    ym, c_m, n_m, m_m = _mlstm(qkv, gates, og, mhg, zc, zn, zm, ym, length=N_META, nseq=1, nsteps=1,
                               row0=ROW_M0, carry=True, name="mlstm_meta")
    ym, p_c, p_n, p_m = _mlstm(qkv, gates, og, mhg, c_m, n_m, m_m, ym, length=M_CHUNK_P, nseq=1,
                               nsteps=SEQ // M_CHUNK_P, row0=0, carry=True, name="mlstm_prompt")
    ym, s_c, s_n, s_m = _mlstm(qkv, gates, og, mhg, state_C[0], state_n[0], state_m[0][..., None], ym,
                               length=DEC_SEQ, nseq=SEQ_PER_STEP, nsteps=DEC_BATCH // SEQ_PER_STEP,
                               row0=ROW_S0, carry=False, name="mlstm_sample")

    wq = w_uq[0].reshape(Q_LORA, A_HEADS, A_NOPE + A_ROPE).transpose(1, 0, 2)
    wq_r = wq[..., A_NOPE:]
    wq_p = jnp.concatenate([wq[..., :A_NOPE], _pad_cols(wq_r, ROPE_PAD), _pad_cols(_rot_cols(wq_r), ROPE_PAD)],
                           axis=-1).astype(BF16)
    wuk = w_uk[0].transpose(1, 2, 0).astype(BF16)
    wuv = w_uv[0].transpose(1, 0, 2).astype(BF16)
    q_lat, q_rope, q_heads = _qproj(qn, wq_p, wuk, cos_t, sin_t)
    wk = w_uk[0].transpose(1, 0, 2).astype(BF16)
    wvt = w_uv[0].transpose(1, 2, 0).astype(BF16)
    k_heads, v_heads_t = _kvproj(ckv_b, kr_b, wk, wvt)
    ya = jnp.zeros((ROWS, A_HEADS * A_VD), BF16)
    ya = _prompt_attn(q_heads, k_heads, v_heads_t, v_heads_t[:, ROW_M0:ROW_M0 + N_META], ya)
    o_s = _paged_attn(page_table, q_lat, q_rope, ckv_b, kr_b, cache_latent[0],
                      jnp.swapaxes(cache_krope[0], 1, 2))
    ya = _uvproj_sample(o_s, wuv, ya)

    t = _merge(ym, ya, w_br_m[0].astype(BF16), w_br_a[0].astype(BF16), og)
    x1f, x1b = _outln(t, w_out[0].astype(BF16), x_p, x_r, ln1_g, ln1_b)
    y = _mlp(x1b, w_up[0].astype(BF16), w_down[0].astype(BF16), x1f, ln2_g, ln2_b)

    def prompt_rows(a, width):
        return jnp.concatenate([a[ROW_M0:ROW_M0 + N_META, :width], a[:SEQ, :width]], axis=0)[None, None]

    def sample_rows(a, width):
        return a[ROW_S0:ROW_M0, :width].reshape(1, DEC_BATCH, DEC_SEQ, width)

    return (y[:SEQ][None], y[ROW_S0:ROW_M0].reshape(DEC_BATCH, DEC_SEQ, D_MODEL),
            prompt_rows(ckv_f, KV_LORA), prompt_rows(kr_f, A_ROPE),
            p_c[None], p_n[None], p_m[..., 0][None],
            sample_rows(ckv_f, KV_LORA), sample_rows(kr_f, A_ROPE),
            s_c[None], s_n[None], s_m[..., 0][None])
```

```python
import functools
import math

import numpy as np
import jax
import jax.numpy as jnp
from jax import lax
from jax.experimental import pallas as pl
from jax.experimental.pallas import tpu as pltpu

F32 = jnp.float32
BF16 = jnp.bfloat16

D_MODEL = 2048
SEQ = 8192
DEC_BATCH = 128
DEC_SEQ = 8
PAST_LEN = 16384
PAGE_SIZE = 128
N_META = 16
M_HEADS = 8
M_DK = 128
M_DV = 256
A_HEADS = 16
A_NOPE = 128
A_ROPE = 64
A_VD = 128
Q_LORA = 512
KV_LORA = 256
ROPE_BASE = 10000.0
ATTN_SCALE = (A_NOPE + A_ROPE) ** -0.5
D_FF = 4 * D_MODEL
ALPHA = 2.0 ** 0.25
LN_EPS = 1e-5
RMS_EPS = 1e-6

OFF_Q, OFF_K, OFF_V, OFF_O, OFF_I, OFF_F, OFF_AQ, OFF_CKV, OFF_KR, OFF_G = (
    0, 1024, 2048, 4096, 6144, 6152, 6160, 6672, 6928, 6992)
D_IN = 11088

ROW_S0 = SEQ
ROW_M0 = SEQ + DEC_BATCH * DEC_SEQ
ROWS = 9728
TM = 512
LANE = 128
ROPE_PAD = LANE

M_CHUNK_P = 512
SEQ_PER_STEP = 2
TQ = 256
TK = 512
PAGES_PER_STEP = 32
RING_SLOTS = 4
KEY_GROUPS = 2
PAGED_LOOKAHEAD = 1
VMEM_LIMIT = 56 * 1024 * 1024


def _cparams(sem):
    return pltpu.CompilerParams(dimension_semantics=sem, vmem_limit_bytes=VMEM_LIMIT)


PROMPT_TILES = SEQ // TM


def _x_specs():
    return [pl.BlockSpec((TM, D_MODEL), lambda i, *_: (jnp.minimum(i, PROMPT_TILES - 1), 0)),
            pl.BlockSpec((TM, D_MODEL), lambda i, *_: (jnp.maximum(i - PROMPT_TILES, 0), 0))]


def _x_tile(xp_ref, xr_ref):
    return jnp.where(pl.program_id(0) < PROMPT_TILES, xp_ref[...], xr_ref[...])


def _mm_kernel(xp_ref, xr_ref, w_ref, b_ref, s_ref, o_ref, xb_ref, *, act):
    is_prompt = pl.program_id(0) < PROMPT_TILES

    @pl.when(pl.program_id(1) == 0)
    def _():
        xb_ref[...] = jnp.where(is_prompt, xp_ref[...], xr_ref[...]).astype(BF16)

    y = jnp.dot(xb_ref[...], w_ref[...], preferred_element_type=F32)
    y = (y + b_ref[...]) * s_ref[...]
    if act == "sigmoid":
        y = jax.nn.sigmoid(y)
    o_ref[...] = y.astype(o_ref.dtype)


def _matmul(xp, xr, w, b, s, *, act, out_dtype, tn, name):
    k, n = w.shape
    return pl.pallas_call(
        functools.partial(_mm_kernel, act=act),
        grid=(ROWS // TM, n // tn),
        in_specs=_x_specs() + [
            pl.BlockSpec((k, tn), lambda i, j: (0, j)),
            pl.BlockSpec((1, tn), lambda i, j: (0, j)),
            pl.BlockSpec((1, tn), lambda i, j: (0, j)),
        ],
        out_specs=pl.BlockSpec((TM, tn), lambda i, j: (i, j)),
        out_shape=jax.ShapeDtypeStruct((ROWS, n), out_dtype),
        scratch_shapes=[pltpu.VMEM((TM, k), BF16)],
        compiler_params=_cparams(("parallel", "arbitrary")),
        name=name,
    )(xp, xr, w, b, s)


T_AQ, T_CKV, T_KR, T_KROT, T_G, T_END = 0, 512, 768, 896, 1024, 1152


def _tail_kernel(xp_ref, xr_ref, w_ref, b_ref, qg_ref, kg_ref, cos_ref, sin_ref,
                 qn_ref, cf_ref, cb_ref, rf_ref, rb_ref, g_ref):
    xb = _x_tile(xp_ref, xr_ref).astype(BF16)
    acc = jnp.dot(xb, w_ref[...], preferred_element_type=F32) + b_ref[...]
    aq = acc[:, T_AQ:T_CKV]
    qn = aq * lax.rsqrt(jnp.mean(aq * aq, axis=-1, keepdims=True) + RMS_EPS) * qg_ref[...]
    qn_ref[...] = qn.astype(BF16)
    ckv = acc[:, T_CKV:T_KR]
    c = ckv * lax.rsqrt(jnp.mean(ckv * ckv, axis=-1, keepdims=True) + RMS_EPS) * kg_ref[...]
    cf_ref[...] = c
    cb_ref[...] = c.astype(BF16)
    kr = acc[:, T_KR:T_KROT] * cos_ref[...] + acc[:, T_KROT:T_G] * sin_ref[...]
    rf_ref[...] = kr
    rb_ref[...] = kr.astype(BF16)
    g_ref[...] = acc[:, T_G:T_END]


def _tail(xp, xr, w, b, qg, kg, cos, sin):
    row = lambda n: pl.BlockSpec((TM, n), lambda i: (i, 0))
    full = lambda a: pl.BlockSpec(a.shape, lambda i: (0,) * a.ndim)
    sd = jax.ShapeDtypeStruct
    return pl.pallas_call(
        _tail_kernel,
        grid=(ROWS // TM,),
        in_specs=_x_specs() + [full(w), full(b), full(qg), full(kg), row(ROPE_PAD), row(ROPE_PAD)],
        out_specs=[row(Q_LORA), row(KV_LORA), row(KV_LORA), row(ROPE_PAD), row(ROPE_PAD), row(LANE)],
        out_shape=[sd((ROWS, Q_LORA), BF16), sd((ROWS, KV_LORA), F32), sd((ROWS, KV_LORA), BF16),
                   sd((ROWS, ROPE_PAD), F32), sd((ROWS, ROPE_PAD), BF16), sd((ROWS, LANE), F32)],
        compiler_params=_cparams(("parallel",)),
        name="inproj_tail",
    )(xp, xr, w, b, qg, kg, cos, sin)


def _log_sigmoid(x):
    return jnp.minimum(x, 0.0) - jnp.log1p(jnp.exp(-jnp.abs(x)))


def _mlstm_head(q, k, v, ig_c, ig_r, lf_c, lf_r, c0, n0, m0, length):
    ti = lax.broadcasted_iota(jnp.int32, (length, length), 0)
    si = lax.broadcasted_iota(jnp.int32, (length, length), 1)
    causal = si <= ti
    b_c = jnp.sum(jnp.where(causal, lf_r, 0.0), axis=1, keepdims=True)
    b_r = jnp.sum(jnp.where(ti <= si, lf_c, 0.0), axis=0, keepdims=True)
    dlog = jnp.where(causal, b_c - b_r + ig_r, -jnp.inf)
    m = jnp.maximum(b_c + m0, jnp.max(dlog, axis=1, keepdims=True))
    w = jnp.exp(dlog - m)
    inter = jnp.exp(b_c + m0 - m)
    s = lax.dot_general(q, k, (((1,), (1,)), ((), ())), preferred_element_type=F32) * w
    qc = jnp.dot(q, c0.astype(BF16), preferred_element_type=F32)
    num = jnp.dot(s.astype(BF16), v, preferred_element_type=F32) + inter * qc
    qn = jnp.sum(q.astype(F32) * n0, axis=1, keepdims=True)
    nq = jnp.sum(s, axis=1, keepdims=True) + inter * qn
    h = num / jnp.maximum(jnp.abs(nq), jnp.exp(-m))
    m_end = m[length - 1:length, :]
    b_end = b_c[length - 1:length, :]
    w_end = jnp.exp(b_end - b_c + ig_c - m_end)
    dec = jnp.exp(b_end + m0 - m_end)
    kw = k.astype(F32) * w_end
    c = dec * c0 + lax.dot_general(kw.astype(BF16), v, (((0,), (0,)), ((), ())),
                                   preferred_element_type=F32)
    n = dec * n0 + jnp.sum(kw, axis=0, keepdims=True)
    return h, c, n, m_end


def _mlstm_kernel(q_ref, k_ref, v_ref, g_ref, og_ref, mhg_ref, c0_ref, n0_ref, m0_ref, yin_ref,
                  y_ref, c_ref, n_ref, m_ref, cs_ref, ns_ref, ms_ref, *, length, nseq, carry):
    del yin_ref
    step = pl.program_id(0)
    if carry:
        @pl.when(step == 0)
        def _():
            cs_ref[...] = c0_ref[...]
            ns_ref[...] = n0_ref[...]
            ms_ref[...] = m0_ref[...]
        c_src, n_src, m_src = cs_ref, ns_ref, ms_ref
    else:
        c_src, n_src, m_src = c0_ref, n0_ref, m0_ref

    def head_stages(b, h, g, gt, lfc_all, lfr_all):
        r0 = b * length
        q = q_ref[r0:r0 + length, h * M_DK:(h + 1) * M_DK]
        k = k_ref[r0:r0 + length, h * M_DK:(h + 1) * M_DK]
        v = v_ref[r0:r0 + length, h * M_DV:(h + 1) * M_DV]
        return _mlstm_head(
            q, k, v,
            g[:, h:h + 1], gt[h:h + 1, :],
            lfc_all[:, M_HEADS + h:M_HEADS + h + 1], lfr_all[M_HEADS + h:M_HEADS + h + 1, :],
            c_src[b, h], n_src[b, h:h + 1, :], m_src[b, h:h + 1, :], length)

    def readout(b, h, result):
        hh, c, n, m_end = result
        r0 = b * length
        mu = jnp.mean(hh, axis=1, keepdims=True)
        d = hh - mu
        var = jnp.mean(d * d, axis=1, keepdims=True)
        hn = d * lax.rsqrt(var + LN_EPS)
        cols = slice(h * M_DV, (h + 1) * M_DV)
        y_ref[r0:r0 + length, cols] = (hn * mhg_ref[:, cols] * og_ref[r0:r0 + length, cols]).astype(BF16)
        c_dst, n_dst, m_dst = (cs_ref, ns_ref, ms_ref) if carry else (c_ref, n_ref, m_ref)
        c_dst[b, h] = c
        n_dst[b, h:h + 1, :] = n
        m_dst[b, h:h + 1, :] = m_end

    for b in range(nseq):
        r0 = b * length
        g = g_ref[r0:r0 + length, :]
        gt = g.T
        lfc_all = _log_sigmoid(g)
        lfr_all = _log_sigmoid(gt)
        for h in range(M_HEADS):
            readout(b, h, head_stages(b, h, g, gt, lfc_all, lfr_all))

    if carry:
        @pl.when(step == pl.num_programs(0) - 1)
        def _():
            c_ref[...] = cs_ref[...]
            n_ref[...] = ns_ref[...]
            m_ref[...] = ms_ref[...]


def _mlstm(qkv, gates, og, mhg, c0, n0, m0, y_in, *, length, nseq, nsteps, row0, carry, name):
    rows = nseq * length
    rb0 = row0 // rows
    nstate = c0.shape[0]
    if carry:
        st = lambda *tail: pl.BlockSpec((nstate,) + tail, lambda s: (0,) * (1 + len(tail)))
    else:
        st = lambda *tail: pl.BlockSpec((nseq,) + tail, lambda s: (s,) + (0,) * len(tail))
    sd = jax.ShapeDtypeStruct
    return pl.pallas_call(
        functools.partial(_mlstm_kernel, length=length, nseq=nseq, carry=carry),
        grid=(nsteps,),
        in_specs=[
            pl.BlockSpec((rows, M_HEADS * M_DK), lambda s: (rb0 + s, 0)),
            pl.BlockSpec((rows, M_HEADS * M_DK), lambda s: (rb0 + s, 1)),
            pl.BlockSpec((rows, M_HEADS * M_DV), lambda s: (rb0 + s, 1)),
            pl.BlockSpec((rows, LANE), lambda s: (rb0 + s, 0)),
            pl.BlockSpec((rows, M_HEADS * M_DV), lambda s: (rb0 + s, 0)),
            pl.BlockSpec((1, M_HEADS * M_DV), lambda s: (0, 0)),
            st(M_HEADS, M_DK, M_DV), st(M_HEADS, M_DK), st(M_HEADS, 1),
            pl.BlockSpec(memory_space=pl.ANY),
        ],
        out_specs=[
            pl.BlockSpec((rows, M_HEADS * M_DV), lambda s: (rb0 + s, 0)),
            st(M_HEADS, M_DK, M_DV), st(M_HEADS, M_DK), st(M_HEADS, 1),
        ],
        out_shape=[sd(y_in.shape, BF16), sd(c0.shape, F32), sd(n0.shape, F32), sd(m0.shape, F32)],
        scratch_shapes=[pltpu.VMEM((nstate, M_HEADS, M_DK, M_DV), F32),
                        pltpu.VMEM((nstate, M_HEADS, M_DK), F32),
                        pltpu.VMEM((nstate, M_HEADS, 1), F32)] if carry else
                       [pltpu.VMEM((8, LANE), F32)] * 3,
        input_output_aliases={9: 0},
        compiler_params=_cparams(("arbitrary",)),
        name=name,
    )(qkv, qkv, qkv, gates, og, mhg, c0, n0, m0, y_in)


QW = A_NOPE + 2 * ROPE_PAD


HEAD_W = A_NOPE + ROPE_PAD


def _qproj_kernel(qn_ref, wq_ref, wuk_ref, cos_ref, sin_ref, ql_ref, qr_ref, qh_ref):
    qn = qn_ref[...]
    for h in range(A_HEADS):
        q = jnp.dot(qn, wq_ref[h], preferred_element_type=F32)
        q_nope = q[:, :A_NOPE].astype(BF16)
        ql = jnp.dot(q_nope, wuk_ref[h], preferred_element_type=F32)
        ql_ref[:, h * KV_LORA:(h + 1) * KV_LORA] = ql.astype(BF16)
        qr_f = q[:, A_NOPE:A_NOPE + ROPE_PAD] * cos_ref[...] + q[:, A_NOPE + ROPE_PAD:] * sin_ref[...]
        qr_ref[:, h * ROPE_PAD:(h + 1) * ROPE_PAD] = qr_f.astype(BF16)
        qh_ref[:, h * HEAD_W:h * HEAD_W + A_NOPE] = (q[:, :A_NOPE] * C_LOG2).astype(BF16)
        qh_ref[:, h * HEAD_W + A_NOPE:(h + 1) * HEAD_W] = (qr_f * C_LOG2).astype(BF16)


def _qproj(qn, wq, wuk, cos, sin):
    sd = jax.ShapeDtypeStruct
    row = lambda n: pl.BlockSpec((TM, n), lambda i: (i, 0))
    full = lambda a: pl.BlockSpec(a.shape, lambda i: (0,) * a.ndim)
    return pl.pallas_call(
        _qproj_kernel,
        grid=(ROWS // TM,),
        in_specs=[row(Q_LORA), full(wq), full(wuk), row(ROPE_PAD), row(ROPE_PAD)],
        out_specs=[row(A_HEADS * KV_LORA), row(A_HEADS * ROPE_PAD), row(A_HEADS * HEAD_W)],
        out_shape=[sd((ROWS, A_HEADS * KV_LORA), BF16), sd((ROWS, A_HEADS * ROPE_PAD), BF16),
                   sd((ROWS, A_HEADS * HEAD_W), BF16)],
        compiler_params=_cparams(("parallel",)),
        name="mla_qproj",
    )(qn, wq, wuk, cos, sin)


def _kvproj_kernel(c_ref, kr_ref, wk_ref, wvt_ref, k_ref, vt_ref):
    c = c_ref[...]
    kr = kr_ref[...]
    for h in range(A_HEADS):
        kn = jnp.dot(c, wk_ref[h], preferred_element_type=F32)
        k_ref[:, h * HEAD_W:h * HEAD_W + A_NOPE] = kn.astype(BF16)
        k_ref[:, h * HEAD_W + A_NOPE:(h + 1) * HEAD_W] = kr
        vt = lax.dot_general(wvt_ref[h], c, NT_DIMS, preferred_element_type=F32)
        vt_ref[h * A_VD:(h + 1) * A_VD, :] = vt.astype(BF16)


def _kvproj(c, kr, wk, wvt):
    sd = jax.ShapeDtypeStruct
    row = lambda n: pl.BlockSpec((TM, n), lambda i: (i, 0))
    full = lambda a: pl.BlockSpec(a.shape, lambda i: (0,) * a.ndim)
    return pl.pallas_call(
        _kvproj_kernel,
        grid=(ROWS // TM,),
        in_specs=[row(KV_LORA), row(ROPE_PAD), full(wk), full(wvt)],
        out_specs=[row(A_HEADS * HEAD_W), pl.BlockSpec((A_HEADS * A_VD, TM), lambda i: (0, i))],
        out_shape=[sd((ROWS, A_HEADS * HEAD_W), BF16), sd((A_HEADS * A_VD, ROWS), BF16)],
        compiler_params=_cparams(("parallel",)),
        name="mla_kvproj",
    )(c, kr, wk, wvt)


def _stack_heads(ref, width):
    return jnp.concatenate([ref[:, h * width:(h + 1) * width] for h in range(A_HEADS)], axis=0)


NT_DIMS = (((1,), (1,)), ((), ()))
C_LOG2 = ATTN_SCALE * math.log2(math.e)
SCORE_LOOKAHEAD = 4


def _softmax_update_t(st, vt, g, m_ref, l_ref, acc_ref, init):
    mx = jnp.max(st, axis=0, keepdims=True)
    if init:
        m_new = mx
        p = jnp.exp2(st - m_new)
        l_ref[g] = jnp.sum(p, axis=0, keepdims=True)
        acc_ref[g] = jnp.dot(vt, p.astype(BF16), preferred_element_type=F32)
    else:
        m_old = m_ref[g]
        m_new = jnp.maximum(m_old, mx)
        alpha = jnp.exp2(m_old - m_new)
        p = jnp.exp2(st - m_new)
        l_ref[g] = alpha * l_ref[g] + jnp.sum(p, axis=0, keepdims=True)
        acc_ref[g] = alpha * acc_ref[g] + jnp.dot(vt, p.astype(BF16), preferred_element_type=F32)
    m_ref[g] = m_new


def _prompt_attn_kernel(qi_ref, kj_ref, first_ref, last_ref, qpos_ref,
                        q_ref, k_ref, vt_ref, mk_ref, mvt_ref, yin_ref,
                        y_ref, m_ref, l_ref, acc_ref):
    del yin_ref
    s_idx = pl.program_id(0)
    qpos = qpos_ref[s_idx] + lax.broadcasted_iota(jnp.int32, (1, TQ), 1)

    def scores(keys_ref, h):
        cols = slice(h * HEAD_W, (h + 1) * HEAD_W)
        return lax.dot_general(keys_ref[:, cols], q_ref[:, cols], NT_DIMS,
                               preferred_element_type=F32)

    def sweep(keys_ref, v_t_ref, kpos, init):
        ahead = [scores(keys_ref, h) for h in range(SCORE_LOOKAHEAD)]
        for h in range(A_HEADS):
            st = ahead.pop(0)
            if h + SCORE_LOOKAHEAD < A_HEADS:
                ahead.append(scores(keys_ref, h + SCORE_LOOKAHEAD))
            if kpos is not None:
                st = jnp.where(kpos <= qpos, st, -jnp.inf)
            _softmax_update_t(st, v_t_ref[h * A_VD:(h + 1) * A_VD, :], h, m_ref, l_ref, acc_ref, init=init)

    @pl.when(first_ref[s_idx] == 1)
    def _():
        kpos = lax.broadcasted_iota(jnp.int32, (N_META, 1), 0)
        sweep(mk_ref, mvt_ref, kpos, init=True)

    def key_block(masked):
        kpos = N_META + kj_ref[s_idx] * TK + lax.broadcasted_iota(jnp.int32, (TK, 1), 0)
        sweep(k_ref, vt_ref, kpos if masked else None, init=False)

    pl.when(last_ref[s_idx] == 0)(functools.partial(key_block, False))
    pl.when(last_ref[s_idx] == 1)(functools.partial(key_block, True))

    @pl.when(last_ref[s_idx] == 1)
    def _():
        for h in range(A_HEADS):
            o = (acc_ref[h] / l_ref[h]).T
            y_ref[:, h * A_VD:(h + 1) * A_VD] = o.astype(BF16)


def _prompt_schedule():
    qi, kj, first, last, qpos = [], [], [], [], []
    for i in range(SEQ // TQ):
        nk = (i * TQ + TQ - 1) // TK + 1
        for j in range(nk):
            qi.append(i); kj.append(j); first.append(int(j == 0)); last.append(int(j == nk - 1))
            qpos.append(N_META + i * TQ)
    qi.append(ROW_M0 // TQ); kj.append(0); first.append(1); last.append(1); qpos.append(0)
    return [jnp.asarray(np.asarray(a, np.int32)) for a in (qi, kj, first, last, qpos)]


def _prompt_attn(qh, kh, vt, mvt, y_in):
    sched = _prompt_schedule()
    nsteps = int(sched[0].shape[0])
    mblk = ROW_M0 // N_META
    grid_spec = pltpu.PrefetchScalarGridSpec(
        num_scalar_prefetch=5,
        grid=(nsteps,),
        in_specs=[
            pl.BlockSpec((TQ, A_HEADS * HEAD_W), lambda s, qi, kj, f, l, p: (qi[s], 0)),
            pl.BlockSpec((TK, A_HEADS * HEAD_W), lambda s, qi, kj, f, l, p: (kj[s], 0)),
            pl.BlockSpec((A_HEADS * A_VD, TK), lambda s, qi, kj, f, l, p: (0, kj[s])),
            pl.BlockSpec((N_META, A_HEADS * HEAD_W), lambda s, qi, kj, f, l, p: (mblk, 0)),
            pl.BlockSpec(mvt.shape, lambda s, qi, kj, f, l, p: (0, 0)),
            pl.BlockSpec(memory_space=pl.ANY),
        ],
        out_specs=pl.BlockSpec((TQ, A_HEADS * A_VD), lambda s, qi, kj, f, l, p: (qi[s], 0)),
        scratch_shapes=[pltpu.VMEM((A_HEADS, 1, TQ), F32), pltpu.VMEM((A_HEADS, 1, TQ), F32),
                        pltpu.VMEM((A_HEADS, A_VD, TQ), F32)],
    )
    return pl.pallas_call(
        _prompt_attn_kernel,
        grid_spec=grid_spec,
        out_shape=jax.ShapeDtypeStruct(y_in.shape, BF16),
        input_output_aliases={5 + 5: 0},
        compiler_params=_cparams(("arbitrary",)),
        name="prompt_attn",
    )(*sched, qh, kh, vt, kh, mvt, y_in)


def _paged_attn_kernel(pt_ref, ql_ref, qr_ref, nl_ref, nr_ref, lat_hbm, rope_hbm, o_ref,
                       lat_buf, rope_buf, sem, qls_ref, qrs_ref, m_ref, l_ref, acc_ref):
    b = pl.program_id(0)
    nseq = pl.num_programs(0)
    ppc = PAGES_PER_STEP
    nch = (PAST_LEN // PAGE_SIZE) // ppc
    ahead = RING_SLOTS - 1

    def page_copies(seq, chunk, slot):
        cps = []
        for p in range(ppc):
            page = pt_ref[seq, chunk * ppc + p]
            cps.append(pltpu.make_async_copy(lat_hbm.at[page], lat_buf.at[slot, p], sem.at[0, slot]))
            cps.append(pltpu.make_async_copy(rope_hbm.at[page], rope_buf.at[slot, p], sem.at[1, slot]))
        return cps

    def start(seq, chunk, slot):
        for cp in page_copies(seq, chunk, slot):
            cp.start()

    @pl.when(b == 0)
    def _():
        for c in range(ahead):
            start(0, c, c)

    qls_ref[...] = _stack_heads(ql_ref, KV_LORA)
    qrs_ref[...] = _stack_heads(qr_ref, ROPE_PAD)[:, :A_ROPE]
    m_ref[...] = jnp.full(m_ref.shape, -jnp.inf, F32)
    l_ref[...] = jnp.zeros(l_ref.shape, F32)
    acc_ref[...] = jnp.zeros(acc_ref.shape, F32)

    def update(g, s, vals):
        m_old = m_ref[g]
        m_new = jnp.maximum(m_old, jnp.max(s, axis=1, keepdims=True))
        alpha = jnp.exp2(m_old - m_new)
        p = jnp.exp2(s - m_new)
        m_ref[g] = m_new
        l_ref[g] = alpha * l_ref[g] + jnp.sum(p, axis=1, keepdims=True)
        acc_ref[g] = alpha * acc_ref[g] + jnp.dot(p.astype(BF16), vals, preferred_element_type=F32)

    def chunk_scores(chunk):
        slot = chunk % RING_SLOTS
        for cp in page_copies(b, chunk, slot):
            cp.wait()
        kl = jnp.concatenate([lat_buf[slot, p] for p in range(ppc)], axis=0).astype(BF16)
        krt = jnp.concatenate([rope_buf[slot, p] for p in range(ppc)], axis=1).astype(BF16)
        s = (lax.dot_general(qls_ref[...], kl, NT_DIMS, preferred_element_type=F32)
             + jnp.dot(qrs_ref[...], krt, preferred_element_type=F32)) * C_LOG2
        return s, kl

    pending = [chunk_scores(c) for c in range(PAGED_LOOKAHEAD)]
    for c in range(nch):
        n = c + ahead
        if n < nch:
            start(b, n, n % RING_SLOTS)
        else:
            pl.when(b + 1 < nseq)(functools.partial(start, b + 1, n - nch, n % RING_SLOTS))
        if c + PAGED_LOOKAHEAD < nch:
            pending.append(chunk_scores(c + PAGED_LOOKAHEAD))
        update(c % KEY_GROUPS, *pending.pop(0))

    def finalize():
        nl = nl_ref[...]
        s2 = (lax.dot_general(qls_ref[...], nl, NT_DIMS, preferred_element_type=F32)
              + lax.dot_general(qrs_ref[...], nr_ref[:, :A_ROPE], NT_DIMS,
                                preferred_element_type=F32)) * C_LOG2
        rows = A_HEADS * DEC_SEQ
        tq = lax.broadcasted_iota(jnp.int32, (rows, 1), 0) & (DEC_SEQ - 1)
        tk = lax.broadcasted_iota(jnp.int32, (1, DEC_SEQ), 1)
        update(0, jnp.where(tk <= tq, s2, -jnp.inf), nl)
        m_tot = m_ref[0]
        for g in range(1, KEY_GROUPS):
            m_tot = jnp.maximum(m_tot, m_ref[g])
        l_tot = jnp.zeros((rows, 1), F32)
        acc = jnp.zeros((rows, KV_LORA), F32)
        for g in range(KEY_GROUPS):
            w = jnp.exp2(m_ref[g] - m_tot)
            l_tot = l_tot + w * l_ref[g]
            acc = acc + w * acc_ref[g]
        o = acc / l_tot
        for h in range(A_HEADS):
            o_ref[:, h * KV_LORA:(h + 1) * KV_LORA] = o[h * DEC_SEQ:(h + 1) * DEC_SEQ, :].astype(BF16)

    finalize()


def _paged_attn(page_table, ql, qr, kl, kr, cache_lat, cache_rope_t):
    ppc = PAGES_PER_STEP
    rb0 = ROW_S0 // DEC_SEQ
    rows = A_HEADS * DEC_SEQ
    assert (PAST_LEN // PAGE_SIZE) % (RING_SLOTS * ppc) == 0
    rowblk = lambda width: pl.BlockSpec((DEC_SEQ, width), lambda b, pt: (rb0 + b, 0))
    grid_spec = pltpu.PrefetchScalarGridSpec(
        num_scalar_prefetch=1,
        grid=(DEC_BATCH,),
        in_specs=[rowblk(A_HEADS * KV_LORA), rowblk(A_HEADS * ROPE_PAD), rowblk(KV_LORA), rowblk(ROPE_PAD),
                  pl.BlockSpec(memory_space=pl.ANY), pl.BlockSpec(memory_space=pl.ANY)],
        out_specs=pl.BlockSpec((DEC_SEQ, A_HEADS * KV_LORA), lambda b, pt: (b, 0)),
        scratch_shapes=[pltpu.VMEM((RING_SLOTS, ppc, PAGE_SIZE, KV_LORA), F32),
                        pltpu.VMEM((RING_SLOTS, ppc, A_ROPE, PAGE_SIZE), F32),
                        pltpu.SemaphoreType.DMA((2, RING_SLOTS)),
                        pltpu.VMEM((rows, KV_LORA), BF16), pltpu.VMEM((rows, A_ROPE), BF16),
                        pltpu.VMEM((KEY_GROUPS, rows, 1), F32), pltpu.VMEM((KEY_GROUPS, rows, 1), F32),
                        pltpu.VMEM((KEY_GROUPS, rows, KV_LORA), F32)],
    )
    return pl.pallas_call(
        _paged_attn_kernel,
        grid_spec=grid_spec,
        out_shape=jax.ShapeDtypeStruct((DEC_BATCH * DEC_SEQ, A_HEADS * KV_LORA), BF16),
        compiler_params=_cparams(("arbitrary",)),
        name="paged_attn",
    )(page_table, ql, qr, kl, kr, cache_lat, cache_rope_t)


def _uvproj_kernel(o_ref, w_ref, yin_ref, y_ref):
    del yin_ref
    for h in range(A_HEADS):
        y = jnp.dot(o_ref[:, h * KV_LORA:(h + 1) * KV_LORA], w_ref[h], preferred_element_type=F32)
        y_ref[:, h * A_VD:(h + 1) * A_VD] = y.astype(BF16)


def _uvproj_sample(o, wuv, y_in):
    rb0 = ROW_S0 // TM
    return pl.pallas_call(
        _uvproj_kernel,
        grid=(o.shape[0] // TM,),
        in_specs=[pl.BlockSpec((TM, A_HEADS * KV_LORA), lambda i: (i, 0)),
                  pl.BlockSpec(wuv.shape, lambda i: (0, 0, 0)),
                  pl.BlockSpec(memory_space=pl.ANY)],
        out_specs=pl.BlockSpec((TM, A_HEADS * A_VD), lambda i: (rb0 + i, 0)),
        out_shape=jax.ShapeDtypeStruct(y_in.shape, BF16),
        input_output_aliases={2: 0},
        compiler_params=_cparams(("parallel",)),
        name="mla_uvproj_sample",
    )(o, wuv, y_in)


def _merge_kernel(ym_ref, ya_ref, wm_ref, wa_ref, gm_ref, ga_ref, t_ref):
    y_m = jnp.dot(ym_ref[...], wm_ref[...], preferred_element_type=F32)
    y_a = jnp.dot(ya_ref[...], wa_ref[...], preferred_element_type=F32)
    t_ref[...] = (gm_ref[...] * y_m + ga_ref[...] * y_a).astype(BF16)


def _merge(ym, ya, wm, wa, og):
    tn = 1024
    nb = D_MODEL // tn
    return pl.pallas_call(
        _merge_kernel,
        grid=(ROWS // TM, nb),
        in_specs=[pl.BlockSpec((TM, D_MODEL), lambda i, j: (i, 0)),
                  pl.BlockSpec((TM, D_MODEL), lambda i, j: (i, 0)),
                  pl.BlockSpec((D_MODEL, tn), lambda i, j: (0, j)),
                  pl.BlockSpec((D_MODEL, tn), lambda i, j: (0, j)),
                  pl.BlockSpec((TM, tn), lambda i, j: (i, nb + j)),
                  pl.BlockSpec((TM, tn), lambda i, j: (i, 2 * nb + j))],
        out_specs=pl.BlockSpec((TM, tn), lambda i, j: (i, j)),
        out_shape=jax.ShapeDtypeStruct((ROWS, D_MODEL), BF16),
        compiler_params=_cparams(("parallel", "arbitrary")),
        name="branch_merge",
    )(ym, ya, wm, wa, og, og)


def _layer_norm(z, g, b):
    mu = jnp.mean(z, axis=-1, keepdims=True)
    d = z - mu
    var = jnp.mean(d * d, axis=-1, keepdims=True)
    return d * lax.rsqrt(var + LN_EPS) * g + b


def _outln_kernel(t_ref, w_ref, xp_ref, xr_ref, g_ref, b_ref, xf_ref, xb_ref):
    mix = jnp.dot(t_ref[...], w_ref[...], preferred_element_type=F32)
    x1 = _layer_norm(ALPHA * _x_tile(xp_ref, xr_ref) + mix, g_ref[...], b_ref[...])
    xf_ref[...] = x1
    xb_ref[...] = x1.astype(BF16)


def _outln(t, w, xp, xr, g, b):
    row = pl.BlockSpec((TM, D_MODEL), lambda i: (i, 0))
    vec = pl.BlockSpec((1, D_MODEL), lambda i: (0, 0))
    sd = jax.ShapeDtypeStruct
    return pl.pallas_call(
        _outln_kernel,
        grid=(ROWS // TM,),
        in_specs=[row, pl.BlockSpec((D_MODEL, D_MODEL), lambda i: (0, 0))] + _x_specs() + [vec, vec],
        out_specs=[row, row],
        out_shape=[sd((ROWS, D_MODEL), F32), sd((ROWS, D_MODEL), BF16)],
        compiler_params=_cparams(("parallel",)),
        name="out_proj_ln",
    )(t, w, xp, xr, g, b)


def _mlp_kernel(xb_ref, wu_ref, wd_ref, xf_ref, g_ref, b_ref, y_ref, acc_ref):
    f = pl.program_id(1)
    h = jnp.dot(xb_ref[...], wu_ref[...], preferred_element_type=F32)
    h = jnp.square(jnp.maximum(h, 0.0)).astype(BF16)
    part = jnp.dot(h, wd_ref[...], preferred_element_type=F32)

    @pl.when(f == 0)
    def _():
        acc_ref[...] = part

    @pl.when(f > 0)
    def _():
        acc_ref[...] += part

    @pl.when(f == pl.num_programs(1) - 1)
    def _():
        y_ref[...] = _layer_norm(ALPHA * xf_ref[...] + acc_ref[...], g_ref[...], b_ref[...])


def _mlp(xb, wu, wd, xf, g, b):
    tf = 1024
    row = pl.BlockSpec((TM, D_MODEL), lambda i, f: (i, 0))
    vec = pl.BlockSpec((1, D_MODEL), lambda i, f: (0, 0))
    return pl.pallas_call(
        _mlp_kernel,
        grid=(ROWS // TM, D_FF // tf),
        in_specs=[row, pl.BlockSpec((D_MODEL, tf), lambda i, f: (0, f)),
                  pl.BlockSpec((tf, D_MODEL), lambda i, f: (f, 0)), row, vec, vec],
        out_specs=row,
        out_shape=jax.ShapeDtypeStruct((ROWS, D_MODEL), F32),
        scratch_shapes=[pltpu.VMEM((TM, D_MODEL), F32)],
        compiler_params=_cparams(("parallel", "arbitrary")),
        name="mlp_ln",
    )(xb, wu, wd, xf, g, b)


def _rot_cols(w):
    half = A_ROPE // 2
    return jnp.concatenate([-w[..., half:], w[..., :half]], axis=-1)


def _pad_cols(w, n):
    return jnp.pad(w, [(0, 0)] * (w.ndim - 1) + [(0, n - w.shape[-1])])


def _rope_tables():
    half = A_ROPE // 2
    pos = jnp.concatenate([
        N_META + jnp.arange(SEQ, dtype=jnp.int32),
        jnp.tile(PAST_LEN + jnp.arange(DEC_SEQ, dtype=jnp.int32), DEC_BATCH),
        jnp.arange(N_META, dtype=jnp.int32),
        jnp.zeros((ROWS - ROW_M0 - N_META,), jnp.int32)])
    inv = ROPE_BASE ** (-jnp.arange(half, dtype=F32) / half)
    ang = pos.astype(F32)[:, None] * inv
    cos, sin = jnp.cos(ang), jnp.sin(ang)
    return (_pad_cols(jnp.concatenate([cos, cos], axis=-1), ROPE_PAD),
            _pad_cols(jnp.concatenate([sin, sin], axis=-1), ROPE_PAD))


def kernel(x_prompt, x_sample, cache_latent, cache_krope, state_C, state_n, state_m, page_table, meta, w_in, b_in, mh_g, q_norm_g, kv_norm_g, w_uq, w_uk, w_uv, w_br_m, w_br_a, w_out, ln1_g, ln1_b, w_up, w_down, ln2_g, ln2_b):
    wi, bi = w_in[0], b_in[0]
    x_p = x_prompt[0]
    x_r = jnp.concatenate([
        x_sample.reshape(DEC_BATCH * DEC_SEQ, D_MODEL), meta,
        jnp.zeros((ROWS - ROW_M0 - N_META, D_MODEL), F32)], axis=0)
    cos_t, sin_t = _rope_tables()

    ones = lambda n: jnp.ones((1, n), F32)
    k_scale = jnp.concatenate([jnp.ones((1, 1024), F32), jnp.full((1, 1024), M_DK ** -0.5, F32),
                               jnp.ones((1, 2048), F32)], axis=1)
    qkv = _matmul(x_p, x_r, wi[:, OFF_Q:OFF_O].astype(BF16), bi[None, OFF_Q:OFF_O], k_scale,
                  act=None, out_dtype=BF16, tn=2048, name="inproj_qkv")
    w_og = jnp.concatenate([wi[:, OFF_O:OFF_I], wi[:, OFF_G:]], axis=1).astype(BF16)
    b_og = jnp.concatenate([bi[OFF_O:OFF_I], bi[OFF_G:]])[None]
    og = _matmul(x_p, x_r, w_og, b_og, ones(w_og.shape[1]), act="sigmoid", out_dtype=F32, tn=2048,
                 name="inproj_gates")

    def tail_cols(a):
        kr = a[..., OFF_KR:OFF_G]
        return jnp.concatenate([
            a[..., OFF_AQ:OFF_KR], _pad_cols(kr, ROPE_PAD), _pad_cols(_rot_cols(kr), ROPE_PAD),
            _pad_cols(a[..., OFF_I:OFF_AQ], LANE)], axis=-1)

    qn, ckv_f, ckv_b, kr_f, kr_b, gates = _tail(
        x_p, x_r, tail_cols(wi).astype(BF16), tail_cols(bi)[None], q_norm_g, kv_norm_g, cos_t, sin_t)

    mhg = mh_g
    ym = jnp.zeros((ROWS, M_HEADS * M_DV), BF16)
    zc = jnp.zeros((1, M_HEADS, M_DK, M_DV), F32)
    zn = jnp.zeros((1, M_HEADS, M_DK), F32)
    zm = jnp.zeros((1, M_HEADS, 1), F32)
    ym, c_m, n_m, m_m = _mlstm(qkv, gates, og, mhg, zc, zn, zm, ym, length=N_META, nseq=1, nsteps=1,
                               row0=ROW_M0, carry=True, name="mlstm_meta")
    ym, p_c, p_n, p_m = _mlstm(qkv, gates, og, mhg, c_m, n_m, m_m, ym, length=M_CHUNK_P, nseq=1,
                               nsteps=SEQ // M_CHUNK_P, row0=0, carry=True, name="mlstm_prompt")
    ym, s_c, s_n, s_m = _mlstm(qkv, gates, og, mhg, state_C[0], state_n[0], state_m[0][..., None], ym,
                               length=DEC_SEQ, nseq=SEQ_PER_STEP, nsteps=DEC_BATCH // SEQ_PER_STEP,
                               row0=ROW_S0, carry=False, name="mlstm_sample")

    wq = w_uq[0].reshape(Q_LORA, A_HEADS, A_NOPE + A_ROPE).transpose(1, 0, 2)
    wq_r = wq[..., A_NOPE:]
    wq_p = jnp.concatenate([wq[..., :A_NOPE], _pad_cols(wq_r, ROPE_PAD), _pad_cols(_rot_cols(wq_r), ROPE_PAD)],
                           axis=-1).astype(BF16)
    wuk = w_uk[0].transpose(1, 2, 0).astype(BF16)
    wuv = w_uv[0].transpose(1, 0, 2).astype(BF16)
    q_lat, q_rope, q_heads = _qproj(qn, wq_p, wuk, cos_t, sin_t)
    wk = w_uk[0].transpose(1, 0, 2).astype(BF16)
    wvt = w_uv[0].transpose(1, 2, 0).astype(BF16)
    k_heads, v_heads_t = _kvproj(ckv_b, kr_b, wk, wvt)
    ya = jnp.zeros((ROWS, A_HEADS * A_VD), BF16)
    ya = _prompt_attn(q_heads, k_heads, v_heads_t, v_heads_t[:, ROW_M0:ROW_M0 + N_META], ya)
    o_s = _paged_attn(page_table, q_lat, q_rope, ckv_b, kr_b, cache_latent[0],
                      jnp.swapaxes(cache_krope[0], 1, 2))
    ya = _uvproj_sample(o_s, wuv, ya)

    t = _merge(ym, ya, w_br_m[0].astype(BF16), w_br_a[0].astype(BF16), og)
    x1f, x1b = _outln(t, w_out[0].astype(BF16), x_p, x_r, ln1_g, ln1_b)
    y = _mlp(x1b, w_up[0].astype(BF16), w_down[0].astype(BF16), x1f, ln2_g, ln2_b)

    def prompt_rows(a, width):
        return jnp.concatenate([a[ROW_M0:ROW_M0 + N_META, :width], a[:SEQ, :width]], axis=0)[None, None]

    def sample_rows(a, width):
        return a[ROW_S0:ROW_M0, :width].reshape(1, DEC_BATCH, DEC_SEQ, width)

    return (y[:SEQ][None], y[ROW_S0:ROW_M0].reshape(DEC_BATCH, DEC_SEQ, D_MODEL),
            prompt_rows(ckv_f, KV_LORA), prompt_rows(kr_f, A_ROPE),
            p_c[None], p_n[None], p_m[..., 0][None],
            sample_rows(ckv_f, KV_LORA), sample_rows(kr_f, A_ROPE),
            s_c[None], s_n[None], s_m[..., 0][None])
```
